```python
import jax, jax.numpy as jnp
from jax import lax
import numpy as np

D_MODEL = 4096
BATCH = 2
SEQ = 8192
DEPTH = 2

GRID_W = 64
CTX_LEN = 256
D_FF = 8192
N_MOD = 9
EPS = 1e-6
ROPE_BASE = 10000.0
FFN_RES = 0.5
MLA_HEADS = 16
MLA_Q_RANK = 1536
MLA_KV_RANK = 512
MLA_NOPE = 128
MLA_ROPE = 64
MLA_V = 128
MLA_SCALE = (MLA_NOPE + MLA_ROPE) ** -0.5
Q_BLOCK = 128
LRU_WIDTH = 2048
LRU_BLOCKS = 16
LRU_BW = LRU_WIDTH // LRU_BLOCKS
LRU_C = 8.0
CONV_W = 4
CONV_PAD_L = 2
RET_HEADS = 4
RET_DK = 256
RET_DV = 512
GLA_HEADS = 4
GLA_DK = 256
GLA_DV = 512
GLA_RANK = 16
GLA_TAU = 16.0
CHUNK = 64
EV_SIZES = (MLA_Q_RANK, MLA_KV_RANK, MLA_ROPE, LRU_WIDTH, LRU_WIDTH)
OD_SIZES = (RET_HEADS * RET_DK, RET_HEADS * RET_DK, RET_HEADS * RET_DV, RET_HEADS * RET_DV,
            GLA_HEADS * GLA_DK, GLA_HEADS * GLA_DK, GLA_HEADS * GLA_DV, GLA_HEADS * GLA_DV, 2 * GLA_RANK)
EV_IN = sum(EV_SIZES)
OD_IN = sum(OD_SIZES)
EV_MIX = MLA_HEADS * MLA_V + LRU_WIDTH
OD_MIX = RET_HEADS * RET_DV + GLA_HEADS * GLA_DV
N_EVEN = (DEPTH + 1) // 2
N_ODD = DEPTH // 2

kernel_name = 'hybrid_mla_rglru_retention_gla_dit'


def rmsnorm(x, w):
    x32 = x.astype(jnp.float32)
    x32 = x32 * lax.rsqrt(jnp.mean(x32 * x32, axis=-1, keepdims=True) + EPS)
    return x32.astype(x.dtype) * w


def modulate(h, gain, m, k):
    return rmsnorm(h, gain) * (1.0 + m[k, 1]) + m[k, 0]


def swiglu(u, w_gate, w_up, w_down):
    return (jax.nn.silu(u @ w_gate) * (u @ w_up)) @ w_down


def split_cols(p, sizes):
    return jnp.split(p, np.cumsum(sizes)[:-1].tolist(), axis=-1)


def axial_rope(n_tok, rot_dim):
    rows = n_tok // GRID_W
    row = jnp.repeat(jnp.arange(rows, dtype=jnp.float32), GRID_W)
    col = jnp.tile(jnp.arange(GRID_W, dtype=jnp.float32), rows)
    n_freq = rot_dim // 4
    inv = ROPE_BASE ** (-jnp.arange(n_freq, dtype=jnp.float32) / n_freq)
    ang = jnp.concatenate([row[:, None] * inv, col[:, None] * inv], axis=-1)
    return jnp.cos(ang), jnp.sin(ang)


def apply_rope(x, cos, sin):
    cos, sin = cos.astype(x.dtype), sin.astype(x.dtype)
    x1, x2 = jnp.split(x, 2, axis=-1)
    return jnp.concatenate([x1 * cos - x2 * sin, x1 * sin + x2 * cos], axis=-1)


def dwconv_centred(x, w, b):
    T = x.shape[1]
    xp = jnp.pad(x, ((0, 0), (CONV_PAD_L, CONV_W - 1 - CONV_PAD_L), (0, 0)))
    out = b
    for k in range(CONV_W):
        out = out + xp[:, k:k + T] * w[k]
    return out


def mla_attend(q_n, q_r, k_n, k_r, v):
    s = jnp.einsum('bqhd,bkhd->bhqk', q_n, k_n) + jnp.einsum('bqhr,bkr->bhqk', q_r, k_r)
    p = jax.nn.softmax(s.astype(jnp.float32) * MLA_SCALE, axis=-1).astype(v.dtype)
    return jnp.einsum('bhqk,bkhd->bqhd', p, v)


def mla_blocked(q_n, q_r, k_n, k_r, v):
    B, T, H, _ = q_n.shape
    nb = T // Q_BLOCK
    to_blocks = lambda t: t.reshape(B, nb, Q_BLOCK, *t.shape[2:]).swapaxes(0, 1)
    out = lax.map(lambda qb: mla_attend(qb[0], qb[1], k_n, k_r, v), (to_blocks(q_n), to_blocks(q_r)))
    return out.swapaxes(0, 1).reshape(B, T, H, v.shape[-1])


def linear_scan(a, b, h0, reverse):
    idx = -1 if reverse else 0
    b = b.at[:, idx].add(a[:, idx] * h0)
    def comb(l, r):
        return (l[0] * r[0], r[0] * l[1] + r[1])
    h = lax.associative_scan(comb, (a, b), reverse=reverse, axis=1)[1]
    return h, h[:, idx]


def rglru_gates(xb, w_a, b_a, w_i, b_i, lam):
    B, T, _ = xb.shape
    xg = xb.reshape(B, T, LRU_BLOCKS, LRU_BW)
    r = jax.nn.sigmoid(jnp.einsum('btgi,gij->btgj', xg, w_a).reshape(B, T, LRU_WIDTH) + b_a)
    i = jax.nn.sigmoid(jnp.einsum('btgi,gij->btgj', xg, w_i).reshape(B, T, LRU_WIDTH) + b_i)
    log_a = -LRU_C * r * jax.nn.softplus(-lam)
    return jnp.exp(log_a), jnp.sqrt(-jnp.expm1(2.0 * log_a)) * (i * xb)


def rglru_prefix_bidir(xb_c, xb_x, w_a, b_a, w_i, b_i, lam):
    zero = jnp.zeros((xb_c.shape[0], LRU_WIDTH), xb_c.dtype)
    y_c, y_x = 0.0, 0.0
    for d, rev in enumerate((False, True)):
        a_c, u_c = rglru_gates(xb_c, w_a[d], b_a[d], w_i[d], b_i[d], lam[d])
        h_c, s_c = linear_scan(a_c, u_c, zero, rev)
        a_x, u_x = rglru_gates(xb_x, w_a[d], b_a[d], w_i[d], b_i[d], lam[d])
        h_x, _ = linear_scan(a_x, u_x, s_c, rev)
        y_c = y_c + h_c
        y_x = y_x + h_x
    return y_x, y_c


def chunked_gated_scan(q, k, v, log_a, s0):
    B, H, T, DK = q.shape
    DV = v.shape[-1]
    out_dtype = v.dtype
    n = T // CHUNK
    f32 = jnp.float32
    q = q.astype(f32).reshape(B, H, n, CHUNK, DK)
    k = k.astype(f32).reshape(B, H, n, CHUNK, DK)
    v = v.astype(f32).reshape(B, H, n, CHUNK, DV)
    b = jnp.cumsum(jnp.broadcast_to(log_a.astype(f32), (B, H, T, DK)).reshape(B, H, n, CHUNK, DK), axis=3)
    b_end = b[:, :, :, -1:]
    q_in = q * jnp.exp(b)
    k_in = k * jnp.exp(-b)
    k_st = k * jnp.exp(b_end - b)
    mask = jnp.tril(jnp.ones((CHUNK, CHUNK), bool))
    att = jnp.where(mask, jnp.einsum('bhnid,bhnjd->bhnij', q_in, k_in), 0.0)
    o_intra = jnp.einsum('bhnij,bhnjv->bhniv', att, v)

    def step(s, xs):
        q_c, k_c, v_c, dec_c = xs
        o_c = jnp.einsum('bhid,bhdv->bhiv', q_c, s)
        s = dec_c[..., None] * s + jnp.einsum('bhjd,bhjv->bhdv', k_c, v_c)
        return s, o_c

    xs = tuple(jnp.moveaxis(t, 2, 0) for t in (q_in, k_st, v, jnp.exp(b_end[:, :, :, 0])))
    s_fin, o_inter = lax.scan(step, s0.astype(f32), xs)
    o = o_intra + jnp.moveaxis(o_inter, 0, 2)
    return o.reshape(B, H, T, DV).astype(out_dtype), s_fin


def gated_scan_dir(q, k, v, log_a, s0, reverse):
    if reverse:
        q, k, v, log_a = (jnp.flip(t, axis=2) for t in (q, k, v, log_a))
    o, s = chunked_gated_scan(q, k, v, log_a, s0)
    return (jnp.flip(o, axis=2) if reverse else o), s


def linear_attn_prefix_bidir(qkv_c, qkv_x, la_c, la_x):
    q_c, k_c, v_c = qkv_c
    zero = jnp.zeros((q_c.shape[0], q_c.shape[1], q_c.shape[3], v_c.shape[3]), jnp.float32)
    y_c, y_x = 0.0, 0.0
    for d, rev in enumerate((False, True)):
        o_c, s_c = gated_scan_dir(q_c, k_c, v_c, la_c[d], zero, rev)
        o_x, _ = gated_scan_dir(qkv_x[0], qkv_x[1], qkv_x[2], la_x[d], s_c, rev)
        y_c = y_c + o_c
        y_x = y_x + o_x
    return y_x, y_c


def head_rmsnorm(o, gain):
    B, H, T, dv = o.shape
    o32 = o.astype(jnp.float32)
    o32 = o32 * lax.rsqrt(jnp.mean(o32 * o32, axis=-1, keepdims=True) + EPS)
    return o32.astype(o.dtype).transpose(0, 2, 1, 3).reshape(B, T, H * dv) * gain


def even_mixer(uc, ux, w_in, q_norm, w_uq, kv_norm, w_ukv, conv_w, conv_b, w_a, b_a, w_i, b_i, lam, w_out, with_ctx):
    cos, sin = axial_rope(ux.shape[1], MLA_ROPE)

    def project(u, rope):
        B, T, _ = u.shape
        c_q, c_kv, k_r, xb, gb = split_cols(u @ w_in, EV_SIZES)
        q = (rmsnorm(c_q, q_norm) @ w_uq).reshape(B, T, MLA_HEADS, MLA_NOPE + MLA_ROPE)
        q_n, q_r = q[..., :MLA_NOPE], q[..., MLA_NOPE:]
        kv = (rmsnorm(c_kv, kv_norm) @ w_ukv).reshape(B, T, MLA_HEADS, MLA_NOPE + MLA_V)
        k_n, v = kv[..., :MLA_NOPE], kv[..., MLA_NOPE:]
        if rope:
            q_r = apply_rope(q_r, cos[:, None], sin[:, None])
            k_r = apply_rope(k_r, cos, sin)
        return q_n, q_r, k_n, k_r, v, dwconv_centred(xb, conv_w, conv_b), gb

    qn_c, qr_c, kn_c, kr_c, v_c, xb_c, gb_c = project(uc, False)
    qn_x, qr_x, kn_x, kr_x, v_x, xb_x, gb_x = project(ux, True)
    B, T = ux.shape[0], ux.shape[1]
    att_x = mla_blocked(qn_x, qr_x, jnp.concatenate([kn_c, kn_x], axis=1),
                        jnp.concatenate([kr_c, kr_x], axis=1), jnp.concatenate([v_c, v_x], axis=1))
    lru_x, lru_c = rglru_prefix_bidir(xb_c, xb_x, w_a, b_a, w_i, b_i, lam)
    y_x = jnp.concatenate([att_x.reshape(B, T, MLA_HEADS * MLA_V), lru_x * jax.nn.gelu(gb_x)], axis=-1) @ w_out
    if not with_ctx:
        return y_x, None
    att_c = mla_attend(qn_c, qr_c, kn_c, kr_c, v_c)
    y_c = jnp.concatenate([att_c.reshape(B, uc.shape[1], MLA_HEADS * MLA_V), lru_c * jax.nn.gelu(gb_c)], axis=-1) @ w_out
    return y_x, y_c


def odd_mixer(uc, ux, w_in, ret_log_decay, ret_norm, gla_w_gate2, gla_b_gate, gla_norm, w_out, with_ctx):
    cos, sin = axial_rope(ux.shape[1], RET_DK)

    def project(u, rope):
        B, T, _ = u.shape
        rq, rk, rv, rg, gq, gk, gv, gr, ga = split_cols(u @ w_in, OD_SIZES)
        heads = lambda t, h: t.reshape(B, T, h, -1)
        bhtd = lambda t: t.transpose(0, 2, 1, 3)
        rq = heads(rq, RET_HEADS)
        rk = heads(rk, RET_HEADS) * RET_DK ** -0.5
        if rope:
            rq = apply_rope(rq, cos[:, None], sin[:, None])
            rk = apply_rope(rk, cos[:, None], sin[:, None])
        ret_qkv = (bhtd(rq), bhtd(rk), bhtd(heads(rv, RET_HEADS)))
        gla_qkv = (bhtd(heads(gq, GLA_HEADS) * GLA_DK ** -0.5), bhtd(heads(gk, GLA_HEADS)), bhtd(heads(gv, GLA_HEADS)))
        gla_la = tuple(
            bhtd(heads(jax.nn.log_sigmoid(ga[..., d * GLA_RANK:(d + 1) * GLA_RANK] @ gla_w_gate2[d] + gla_b_gate[d]) / GLA_TAU,
                       GLA_HEADS))
            for d in range(2))
        return ret_qkv, rg, gla_qkv, gla_la, gr

    ret_c, rg_c, gla_c, la_c, gr_c = project(uc, False)
    ret_x, rg_x, gla_x, la_x, gr_x = project(ux, True)
    ret_la = tuple(ret_log_decay[d].reshape(1, RET_HEADS, 1, 1) for d in range(2))
    ro_x, ro_c = linear_attn_prefix_bidir(ret_c, ret_x, ret_la, ret_la)
    go_x, go_c = linear_attn_prefix_bidir(gla_c, gla_x, la_c, la_x)

    def merge(ro, rg, go, gr):
        return jnp.concatenate([head_rmsnorm(ro, ret_norm) * jax.nn.silu(rg),
                                head_rmsnorm(go, gla_norm) * jax.nn.silu(gr)], axis=-1) @ w_out

    y_x = merge(ro_x, rg_x, go_x, gr_x)
    if not with_ctx:
        return y_x, None
    return y_x, merge(ro_c, rg_c, go_c, gr_c)


def setup_inputs(seed: int = 0) -> dict:
    key = jax.random.key(seed)
    ks = iter(jax.random.split(key, 40))
    f32 = jnp.float32

    def nrm(shape, scale):
        return jax.random.normal(next(ks), shape, f32) * scale

    def gain(shape):
        return 1.0 + nrm(shape, 0.02)

    lam_s = jax.random.uniform(next(ks), (N_EVEN, 2, LRU_WIDTH), f32, 0.9, 0.999) ** (1.0 / LRU_C)
    ret_base = jnp.log1p(-(2.0 ** (-5.0 - jnp.arange(RET_HEADS, dtype=f32))))
    return {
        'x': nrm((BATCH, SEQ, D_MODEL), 1.0),
        'c': nrm((BATCH, D_MODEL), 1.0),
        'ctx': nrm((BATCH, CTX_LEN, D_MODEL), 1.0),
        'c_ctx': nrm((D_MODEL,), 1.0),
        'ada_w': nrm((DEPTH, D_MODEL, N_MOD * D_MODEL), 0.5 * D_MODEL ** -0.5),
        'ada_b': nrm((DEPTH, N_MOD * D_MODEL), 0.02),
        'norm_w': gain((DEPTH, 3, D_MODEL)),
        'ffn_w_gate': nrm((DEPTH, 2, D_MODEL, D_FF), D_MODEL ** -0.5),
        'ffn_w_up': nrm((DEPTH, 2, D_MODEL, D_FF), D_MODEL ** -0.5),
        'ffn_w_down': nrm((DEPTH, 2, D_FF, D_MODEL), D_FF ** -0.5),
        'ev_w_in': nrm((N_EVEN, D_MODEL, EV_IN), D_MODEL ** -0.5),
        'mla_q_norm': gain((N_EVEN, MLA_Q_RANK)),
        'mla_w_uq': nrm((N_EVEN, MLA_Q_RANK, MLA_HEADS * (MLA_NOPE + MLA_ROPE)), MLA_Q_RANK ** -0.5),
        'mla_kv_norm': gain((N_EVEN, MLA_KV_RANK)),
        'mla_w_ukv': nrm((N_EVEN, MLA_KV_RANK, MLA_HEADS * (MLA_NOPE + MLA_V)), MLA_KV_RANK ** -0.5),
        'lru_conv_w': nrm((N_EVEN, CONV_W, LRU_WIDTH), CONV_W ** -0.5),
        'lru_conv_b': nrm((N_EVEN, LRU_WIDTH), 0.02),
        'lru_w_a': nrm((N_EVEN, 2, LRU_BLOCKS, LRU_BW, LRU_BW), LRU_BW ** -0.5),
        'lru_b_a': nrm((N_EVEN, 2, LRU_WIDTH), 0.02),
        'lru_w_i': nrm((N_EVEN, 2, LRU_BLOCKS, LRU_BW, LRU_BW), LRU_BW ** -0.5),
        'lru_b_i': nrm((N_EVEN, 2, LRU_WIDTH), 0.02),
        'lru_lambda': jnp.log(lam_s) - jnp.log1p(-lam_s),
        'ev_w_out': nrm((N_EVEN, EV_MIX, D_MODEL), EV_MIX ** -0.5),
        'od_w_in': nrm((N_ODD, D_MODEL, OD_IN), D_MODEL ** -0.5),
        'ret_log_decay': ret_base * (1.0 + nrm((N_ODD, 2, RET_HEADS), 0.05)),
        'ret_norm': gain((N_ODD, RET_HEADS * RET_DV)),
        'gla_w_gate2': nrm((N_ODD, 2, GLA_RANK, GLA_HEADS * GLA_DK), GLA_RANK ** -0.5),
        'gla_b_gate': nrm((N_ODD, 2, GLA_HEADS * GLA_DK), 0.02),
        'gla_norm': gain((N_ODD, GLA_HEADS * GLA_DV)),
        'od_w_out': nrm((N_ODD, OD_MIX, D_MODEL), OD_MIX ** -0.5),
        'final_norm_w': gain((D_MODEL,)),
    }


def reference(x, c, ctx, c_ctx, ada_w, ada_b, norm_w, ffn_w_gate, ffn_w_up, ffn_w_down,
              ev_w_in, mla_q_norm, mla_w_uq, mla_kv_norm, mla_w_ukv, lru_conv_w, lru_conv_b,
              lru_w_a, lru_b_a, lru_w_i, lru_b_i, lru_lambda, ev_w_out,
              od_w_in, ret_log_decay, ret_norm, gla_w_gate2, gla_b_gate, gla_norm, od_w_out,
              final_norm_w):
    B = x.shape[0]
    hx, hc = x, ctx
    s_lat = jax.nn.silu(c)
    s_ctx = jax.nn.silu(c_ctx)
    for l in range(DEPTH):
        last = l == DEPTH - 1
        mx = (s_lat @ ada_w[l] + ada_b[l]).reshape(B, 3, 3, D_MODEL).transpose(1, 2, 0, 3)[:, :, :, None, :]
        mc = (s_ctx @ ada_w[l] + ada_b[l]).reshape(3, 3, D_MODEL)
        ffn_a = (ffn_w_gate[l, 0], ffn_w_up[l, 0], ffn_w_down[l, 0])
        ffn_b = (ffn_w_gate[l, 1], ffn_w_up[l, 1], ffn_w_down[l, 1])
        hx = hx + FFN_RES * mx[0, 2] * swiglu(modulate(hx, norm_w[l, 0], mx, 0), *ffn_a)
        hc = hc + FFN_RES * mc[0, 2] * swiglu(modulate(hc, norm_w[l, 0], mc, 0), *ffn_a)
        ux = modulate(hx, norm_w[l, 1], mx, 1)
        uc = modulate(hc, norm_w[l, 1], mc, 1)
        if l % 2 == 0:
            e = l // 2
            yx, yc = even_mixer(uc, ux, ev_w_in[e], mla_q_norm[e], mla_w_uq[e], mla_kv_norm[e], mla_w_ukv[e],
                                lru_conv_w[e], lru_conv_b[e], lru_w_a[e], lru_b_a[e], lru_w_i[e], lru_b_i[e],
                                lru_lambda[e], ev_w_out[e], not last)
        else:
            o = l // 2
            yx, yc = odd_mixer(uc, ux, od_w_in[o], ret_log_decay[o], ret_norm[o], gla_w_gate2[o], gla_b_gate[o],
                               gla_norm[o], od_w_out[o], not last)
        hx = hx + mx[1, 2] * yx
        hx = hx + FFN_RES * mx[2, 2] * swiglu(modulate(hx, norm_w[l, 2], mx, 2), *ffn_b)
        if not last:
            hc = hc + mc[1, 2] * yc
            hc = hc + FFN_RES * mc[2, 2] * swiglu(modulate(hc, norm_w[l, 2], mc, 2), *ffn_b)
    return rmsnorm(hx, final_norm_w)
```

```python
import functools
import math

import jax
import jax.numpy as jnp
import numpy as np
from jax import lax
from jax.experimental import pallas as pl
from jax.experimental.pallas import tpu as pltpu

F32 = jnp.float32
BF16 = jnp.bfloat16

V7X_LANES = 128
V7X_SUBLANES = 8
V7X_MXU_DIM = 256
V7X_VMEM_BYTES = 64 * 1024 * 1024
V7X_VMEM_CEILING = 58 * 1024 * 1024

GRID_W = 64
EPS = 1e-6
ROPE_BASE = 10000.0
FFN_RES = 0.5
MLA_HEADS = 16
MLA_NOPE = 128
MLA_ROPE = 64
MLA_V = 128
LRU_BLOCKS = 16
LRU_C = 8.0
CONV_W = 4
CONV_PAD_L = 2
RET_HEADS = 4
RET_DK = 256
RET_DV = 512
GLA_HEADS = 4
GLA_DK = 256
GLA_DV = 512
GLA_RANK = 16
GLA_TAU = 16.0
CHUNK = 64


def _cparams(sem, vmem_est):
    limit = int(min(V7X_VMEM_CEILING, max(vmem_est * 5 // 4 + (4 << 20), 16 << 20)))
    return pltpu.CompilerParams(dimension_semantics=sem, vmem_limit_bytes=limit)


def _pick(n, cap, mult):
    best = None
    for t in range(mult, min(n, cap) + 1, mult):
        if n % t == 0:
            best = t
    assert best is not None, (n, cap, mult)
    return best


class _Rows:
    def __init__(self, batch, seq, ctx):
        self.B, self.T, self.TC = batch, seq, ctx
        self.n_lat = batch * seq
        self.n_ctx = batch * ctx
        self.R = self.n_lat + self.n_ctx

    def tile(self, cap):
        g = math.gcd(self.T, self.n_ctx)
        return _pick(g, cap, V7X_SUBLANES)

    def set_of_tile(self, i, tm):
        n_lat_tiles = self.n_lat // tm
        per_seq = self.T // tm
        return jnp.where(i < n_lat_tiles, 1 + i // per_seq, 0)


def _ada_body(c_ref, w_ref, b_ref, o_ref):
    s = c_ref[...]
    s = (s * jax.nn.sigmoid(s)).astype(BF16)
    w = w_ref[0].astype(BF16)
    o_ref[0] = jnp.dot(s, w, preferred_element_type=F32) + b_ref[0]


def _ada(c_rows, ada_w, ada_b):
    depth, d, n = ada_w.shape
    rows = c_rows.shape[0]
    tn = _pick(n, 512, V7X_LANES)
    est = 2 * d * tn * 4 + d * tn * 2 + 4 * rows * d * 4
    return pl.pallas_call(
        _ada_body,
        grid=(depth, n // tn),
        in_specs=[
            pl.BlockSpec((rows, d), lambda l, j: (0, 0)),
            pl.BlockSpec((1, d, tn), lambda l, j: (l, 0, j)),
            pl.BlockSpec((1, 1, tn), lambda l, j: (l, 0, j)),
        ],
        out_specs=pl.BlockSpec((1, rows, tn), lambda l, j: (l, 0, j)),
        out_shape=jax.ShapeDtypeStruct((depth, rows, n), F32),
        compiler_params=_cparams(("arbitrary", "arbitrary"), est),
    )(c_rows, ada_w, ada_b.reshape(depth, 1, n))


def _norm_body(h_ref, g_ref, m_ref, o_ref, *, k):
    x = h_ref[...].astype(F32)
    ms = jnp.mean(x * x, axis=-1, keepdims=True)
    y = (x * lax.rsqrt(ms + EPS)) * g_ref[...]
    if k is not None:
        y = y * (1.0 + m_ref[0, 3 * k + 1:3 * k + 2, :]) + m_ref[0, 3 * k:3 * k + 1, :]
    o_ref[...] = y.astype(o_ref.dtype)


def _norm_mod(h, gain, mods, k, rows, out_dtype, n_rows=None):
    n_rows = h.shape[0] if n_rows is None else n_rows
    d = h.shape[1]
    tm = rows.tile(256)
    est = 2 * tm * d * (4 + 4) + 2 * tm * d * 4
    return pl.pallas_call(
        functools.partial(_norm_body, k=k),
        grid=(n_rows // tm,),
        in_specs=[
            pl.BlockSpec((tm, d), lambda i: (i, 0)),
            pl.BlockSpec((1, d), lambda i: (0, 0)),
            pl.BlockSpec((1, 9, d), lambda i: (rows.set_of_tile(i, tm), 0, 0)),
        ],
        out_specs=pl.BlockSpec((tm, d), lambda i: (i, 0)),
        out_shape=jax.ShapeDtypeStruct((n_rows, d), out_dtype),
        compiler_params=_cparams(("arbitrary",), est),
    )(h, gain.reshape(1, d), mods)


def _mm_up_body(x_ref, wg_ref, wu_ref, o_ref):
    x = x_ref[...]
    g = jnp.dot(x, wg_ref[...], preferred_element_type=F32)
    u = jnp.dot(x, wu_ref[...], preferred_element_type=F32)
    o_ref[...] = (g * jax.nn.sigmoid(g) * u).astype(o_ref.dtype)


def _ffn_up(u, w_gate, w_up, rows):
    r, k = u.shape
    n = w_gate.shape[1]
    tm = rows.tile(512)
    tn = _pick(n, 512, V7X_LANES)
    est = 2 * tm * k * 2 + 4 * k * tn * 2 + 2 * tm * tn * 2 + 3 * tm * tn * 4
    return pl.pallas_call(
        _mm_up_body,
        grid=(n // tn, r // tm),
        in_specs=[
            pl.BlockSpec((tm, k), lambda j, i: (i, 0)),
            pl.BlockSpec((k, tn), lambda j, i: (0, j)),
            pl.BlockSpec((k, tn), lambda j, i: (0, j)),
        ],
        out_specs=pl.BlockSpec((tm, tn), lambda j, i: (i, j)),
        out_shape=jax.ShapeDtypeStruct((r, n), BF16),
        compiler_params=_cparams(("arbitrary", "arbitrary"), est),
    )(u, w_gate, w_up)


def _mm_proj_body(x_ref, w_ref, o_ref):
    o_ref[...] = jnp.dot(x_ref[...], w_ref[...], preferred_element_type=F32).astype(o_ref.dtype)


def _proj(u, w, rows, tn_cap):
    r, k = u.shape
    n = w.shape[1]
    tm = rows.tile(512)
    tn = _pick(n, tn_cap, V7X_LANES)
    est = 2 * tm * k * 2 + 2 * k * tn * 2 + 2 * tm * tn * 2 + tm * tn * 4
    return pl.pallas_call(
        _mm_proj_body,
        grid=(n // tn, r // tm),
        in_specs=[
            pl.BlockSpec((tm, k), lambda j, i: (i, 0)),
            pl.BlockSpec((k, tn), lambda j, i: (0, j)),
        ],
        out_specs=pl.BlockSpec((tm, tn), lambda j, i: (i, j)),
        out_shape=jax.ShapeDtypeStruct((r, n), BF16),
        compiler_params=_cparams(("arbitrary", "arbitrary"), est),
    )(u, w)


def _mm_res_body(*refs, n_pairs, coef, k):
    h_ref, m_ref, o_ref = refs[2 * n_pairs:]
    acc = None
    for p in range(n_pairs):
        part = jnp.dot(refs[2 * p][...], refs[2 * p + 1][...], preferred_element_type=F32)
        acc = part if acc is None else acc + part
    gate = m_ref[0, 3 * k + 2:3 * k + 3, :]
    o_ref[...] = h_ref[...] + (coef * gate) * acc


def _res_matmul(xs, ws, h, mods, k, coef, rows):
    r, n = h.shape
    tm = rows.tile(512)
    tn = _pick(n, 512, V7X_LANES)
    in_specs, args, est = [], [], 0
    for x, w in zip(xs, ws):
        kk = x.shape[1]
        in_specs += [pl.BlockSpec((tm, kk), lambda j, i: (i, 0)),
                     pl.BlockSpec((kk, tn), lambda j, i: (0, j))]
        args += [x, w]
        est += 2 * tm * kk * 2 + 2 * kk * tn * 2
    in_specs += [pl.BlockSpec((tm, tn), lambda j, i: (i, j)),
                 pl.BlockSpec((1, 9, tn), lambda j, i: (rows.set_of_tile(i, tm), 0, j))]
    est += 6 * tm * tn * 4
    return pl.pallas_call(
        functools.partial(_mm_res_body, n_pairs=len(xs), coef=coef, k=k),
        grid=(n // tn, r // tm),
        in_specs=in_specs,
        out_specs=pl.BlockSpec((tm, tn), lambda j, i: (i, j)),
        out_shape=jax.ShapeDtypeStruct((r, n), F32),
        compiler_params=_cparams(("arbitrary", "arbitrary"), est),
    )(*args, h, mods)


def _rope_pairs(t, c, s1, s2):
    half = MLA_ROPE // 2
    return t * c + pltpu.roll(t, V7X_LANES - half, 1) * s1 + pltpu.roll(t, half, 1) * s2


def _lowrank_body(x_ref, g_ref, w_ref, c_ref, s1_ref, s2_ref, o_ref, *, heads, rope):
    x = x_ref[...].astype(F32)
    ms = jnp.mean(x * x, axis=-1, keepdims=True)
    xn = ((x * lax.rsqrt(ms + EPS)) * g_ref[...]).astype(BF16)
    hw = 2 * V7X_LANES
    for h in range(heads):
        acc = jnp.dot(xn, w_ref[:, h * hw:(h + 1) * hw], preferred_element_type=F32)
        if rope:
            o_ref[:, h * hw:h * hw + V7X_LANES] = acc[:, :V7X_LANES].astype(o_ref.dtype)
            t = _rope_pairs(acc[:, V7X_LANES:], c_ref[...], s1_ref[...], s2_ref[...])
            o_ref[:, h * hw + V7X_LANES:(h + 1) * hw] = t.astype(o_ref.dtype)
        else:
            o_ref[:, h * hw:(h + 1) * hw] = acc.astype(o_ref.dtype)


def _lowrank_up(p, col_block, k, gain, w, tables, rows, heads, rope):
    r = p.shape[0]
    n = w.shape[1]
    tm = rows.tile(512)
    est = 2 * tm * k * 2 + 2 * k * n * 2 + 2 * tm * n * 2 + 6 * tm * V7X_LANES * 4 + tm * k * 8 + 4 * tm * 256 * 4
    tab_spec = pl.BlockSpec((tm, V7X_LANES), lambda i: (i, 0))
    return pl.pallas_call(
        functools.partial(_lowrank_body, heads=heads, rope=rope),
        grid=(r // tm,),
        in_specs=[
            pl.BlockSpec((tm, k), lambda i: (i, col_block)),
            pl.BlockSpec((1, k), lambda i: (0, 0)),
            pl.BlockSpec((k, n), lambda i: (0, 0)),
            tab_spec, tab_spec, tab_spec,
        ],
        out_specs=pl.BlockSpec((tm, n), lambda i: (i, 0)),
        out_shape=jax.ShapeDtypeStruct((r, n), BF16),
        compiler_params=_cparams(("arbitrary",), est),
    )(p, gain.reshape(1, k), w, *tables)


def _kr_rope_body(x_ref, c_ref, s1_ref, s2_ref, o_ref):
    t = _rope_pairs(x_ref[...].astype(F32), c_ref[...], s1_ref[...], s2_ref[...])
    o_ref[...] = t.astype(o_ref.dtype)


def _kr_rope(p, col_block, tables, rows):
    r = p.shape[0]
    tm = rows.tile(512)
    spec = pl.BlockSpec((tm, V7X_LANES), lambda i: (i, 0))
    return pl.pallas_call(
        _kr_rope_body,
        grid=(r // tm,),
        in_specs=[pl.BlockSpec((tm, V7X_LANES), lambda i: (i, col_block)), spec, spec, spec],
        out_specs=spec,
        out_shape=jax.ShapeDtypeStruct((r, V7X_LANES), BF16),
        compiler_params=_cparams(("arbitrary",), 16 * tm * V7X_LANES * 4),
    )(p, *tables)


def _attn_body(*refs, n_ctx, n_lat, tk):
    if n_lat:
        (q_ref, kc_ref, vc_ref, krc_ref, kl_ref, vl_ref, krl_ref,
         o_ref, kcat, m_scr, l_scr, acc_scr) = refs
    else:
        q_ref, kc_ref, vc_ref, krc_ref, o_ref, kcat, m_scr, l_scr, acc_scr = refs
    ln = V7X_LANES

    @pl.when(pl.program_id(2) == 0)
    def _():
        kcat[0:n_ctx, 0:ln] = kc_ref[...]
        kcat[0:n_ctx, ln:2 * ln] = krc_ref[...]
        if n_lat:
            kcat[n_ctx:n_ctx + n_lat, 0:ln] = kl_ref[...]
            kcat[n_ctx:n_ctx + n_lat, ln:2 * ln] = krl_ref[...]

    q = q_ref[...]
    m_scr[...] = jnp.full(m_scr.shape, -jnp.inf, F32)
    l_scr[...] = jnp.zeros(l_scr.shape, F32)
    acc_scr[...] = jnp.zeros(acc_scr.shape, F32)

    def step(k, v):
        s = lax.dot_general(q, k, (((1,), (1,)), ((), ())), preferred_element_type=F32)
        m_prev = m_scr[...]
        m_next = jnp.maximum(m_prev, jnp.max(s, axis=1, keepdims=True))
        p = jnp.exp(s - pltpu.repeat(m_next, s.shape[1] // ln, 1))
        alpha = jnp.exp(m_prev - m_next)
        l_scr[...] = alpha * l_scr[...] + jnp.sum(p, axis=1, keepdims=True)
        acc_scr[...] = alpha * acc_scr[...] + jnp.dot(p.astype(BF16), v, preferred_element_type=F32)
        m_scr[...] = m_next

    step(kcat[0:n_ctx, :], vc_ref[...])
    if n_lat:
        def body(c, carry):
            start = pl.multiple_of(c * tk, tk)
            step(kcat[pl.ds(n_ctx + start, tk), :], vl_ref[pl.ds(start, tk), :])
            return carry
        lax.fori_loop(0, n_lat // tk, body, 0)
    o_ref[...] = (acc_scr[...] / l_scr[...]).astype(o_ref.dtype)


def _attention(q, kv, kr, rows, heads, latent):
    b, t, tc = rows.B, rows.T, rows.TC
    ln = V7X_LANES
    ctx_blk0 = rows.n_lat // tc
    if latent:
        tq = _pick(t, 512, V7X_SUBLANES)
        nq = t // tq
        tk = _pick(t, 1024, ln)
        n_lat = t
    else:
        tq, nq, tk, n_lat = tc, 1, tc, 0
    q_row = (lambda bb, qi: bb * nq + qi) if latent else (lambda bb, qi: ctx_blk0 * (tc // tq) + bb)
    in_specs = [
        pl.BlockSpec((tq, 2 * ln), lambda bb, h, qi: (q_row(bb, qi), h)),
        pl.BlockSpec((tc, ln), lambda bb, h, qi: (ctx_blk0 + bb, 2 * h)),
        pl.BlockSpec((tc, ln), lambda bb, h, qi: (ctx_blk0 + bb, 2 * h + 1)),
        pl.BlockSpec((tc, ln), lambda bb, h, qi: (ctx_blk0 + bb, 0)),
    ]
    args = [q, kv, kv, kr]
    if latent:
        in_specs += [
            pl.BlockSpec((t, ln), lambda bb, h, qi: (bb, 2 * h)),
            pl.BlockSpec((t, ln), lambda bb, h, qi: (bb, 2 * h + 1)),
            pl.BlockSpec((t, ln), lambda bb, h, qi: (bb, 0)),
        ]
        args += [kv, kv, kr]
    nk = tc + n_lat
    est = (2 * tq * 2 * ln * 2 + 6 * tc * ln * 2 + 6 * n_lat * ln * 2 + nk * 2 * ln * 2
           + 2 * tq * ln * 2 + 3 * tq * ln * 4 + 4 * tq * max(tk, tc) * 4)
    return pl.pallas_call(
        functools.partial(_attn_body, n_ctx=tc, n_lat=n_lat, tk=tk),
        grid=(b, heads, nq),
        in_specs=in_specs,
        out_specs=pl.BlockSpec((tq, ln), lambda bb, h, qi: (q_row(bb, qi), h)),
        out_shape=jax.ShapeDtypeStruct((rows.R, heads * ln), BF16),
        scratch_shapes=[
            pltpu.VMEM((nk, 2 * ln), BF16),
            pltpu.VMEM((tq, ln), F32),
            pltpu.VMEM((tq, ln), F32),
            pltpu.VMEM((tq, ln), F32),
        ],
        compiler_params=_cparams(("arbitrary", "arbitrary", "arbitrary"), est),
    )(*args)


def _scan_rows(a, bv, reverse):
    n = a.shape[0]
    row = lax.broadcasted_iota(jnp.int32, a.shape, 0)
    s = 1
    while s < n:
        if reverse:
            keep = row < (n - s)
            a_sh = jnp.where(keep, pltpu.roll(a, n - s, 0), 1.0)
            b_sh = jnp.where(keep, pltpu.roll(bv, n - s, 0), 0.0)
        else:
            keep = row >= s
            a_sh = jnp.where(keep, pltpu.roll(a, s, 0), 1.0)
            b_sh = jnp.where(keep, pltpu.roll(bv, s, 0), 0.0)
        bv = a * b_sh + bv
        a = a * a_sh
        s *= 2
    return a, bv


def _cumsum_rows(x, reverse):
    n = x.shape[0]
    row = lax.broadcasted_iota(jnp.int32, x.shape, 0)
    s = 1
    while s < n:
        if reverse:
            x = x + jnp.where(row < (n - s), pltpu.roll(x, n - s, 0), 0.0)
        else:
            x = x + jnp.where(row >= s, pltpu.roll(x, s, 0), 0.0)
        s *= 2
    return x


def _gelu_tanh(x):
    return 0.5 * x * (1.0 + jnp.tanh(math.sqrt(2.0 / math.pi) * (x + 0.044715 * (x * x * x))))


def _lru_body(xl_ref, xc_ref, gl_ref, gc_ref, cw_ref, cb_ref, wa_ref, ba_ref, wi_ref, bi_ref, lam_ref,
              yl_ref, yc_ref, cv_l, cv_c, *, seq, ctx, chunk, conv_chunk):
    pad = V7X_SUBLANES
    zeros8 = jnp.zeros((pad, V7X_LANES), F32)

    def conv_into(src_ref, dst, n, step):
        for c0 in range(0, n, step):
            lo = zeros8 if c0 == 0 else src_ref[c0 - pad:c0, :].astype(F32)
            hi = zeros8 if c0 + step >= n else src_ref[c0 + step:c0 + step + pad, :].astype(F32)
            ext = jnp.concatenate([lo, src_ref[c0:c0 + step, :].astype(F32), hi], axis=0)
            out = cb_ref[...]
            for kk in range(CONV_W):
                off = pad + kk - CONV_PAD_L
                out = out + ext[off:off + step, :] * cw_ref[kk:kk + 1, :]
            dst[c0:c0 + step, :] = out

    conv_into(xl_ref, cv_l, seq, conv_chunk)
    conv_into(xc_ref, cv_c, ctx, min(ctx, conv_chunk))

    for d, reverse in enumerate((False, True)):
        lam = lam_ref[d:d + 1, :]
        neg_sp = -LRU_C * (jnp.maximum(-lam, 0.0) + jnp.log(1.0 + jnp.exp(-jnp.abs(lam))))
        wa = wa_ref[d, 0]
        wi = wi_ref[d, 0]
        ba = ba_ref[d:d + 1, :]
        bi = bi_ref[d:d + 1, :]

        def block(x, carry):
            xg = x.astype(BF16)
            r = jax.nn.sigmoid(jnp.dot(xg, wa, preferred_element_type=F32) + ba)
            i = jax.nn.sigmoid(jnp.dot(xg, wi, preferred_element_type=F32) + bi)
            a = jnp.exp(neg_sp * r)
            bv = jnp.sqrt(1.0 - a * a) * (i * x)
            a_cum, h = _scan_rows(a, bv, reverse)
            h = a_cum * carry + h
            n = x.shape[0]
            new_carry = h[0:1, :] if reverse else h[n - 1:n, :]
            return h, new_carry

        def emit(dst_ref, g_ref, start, n, h):
            if not reverse:
                dst_ref[pl.ds(start, n), :] = h.astype(dst_ref.dtype)
            else:
                tot = dst_ref[pl.ds(start, n), :].astype(F32) + h
                g = g_ref[pl.ds(start, n), :].astype(F32)
                dst_ref[pl.ds(start, n), :] = (tot * _gelu_tanh(g)).astype(dst_ref.dtype)

        carry = jnp.zeros((1, V7X_LANES), F32)
        cchunk = min(ctx, chunk)
        n_cc = ctx // cchunk
        order = range(n_cc - 1, -1, -1) if reverse else range(n_cc)
        lat_init = None
        for c in order:
            h, carry = block(cv_c[c * cchunk:(c + 1) * cchunk, :], carry)
            if lat_init is None:
                lat_init = h[cchunk - 1:cchunk, :] if reverse else h[0:1, :]
            emit(yc_ref, gc_ref, c * cchunk, cchunk, h)
        carry = lat_init

        n_lc = seq // chunk

        def body(it, carry):
            c = (n_lc - 1 - it) if reverse else it
            start = pl.multiple_of(c * chunk, chunk)
            h, carry = block(cv_l[pl.ds(start, chunk), :], carry)
            emit(yl_ref, gl_ref, start, chunk, h)
            return carry

        lax.fori_loop(0, n_lc, body, carry)


def _rglru(p, xb_col, gb_col, width, conv_w, conv_b, w_a, b_a, w_i, b_i, lam, rows):
    b, t, tc = rows.B, rows.T, rows.TC
    ln = V7X_LANES
    groups = width // ln
    ctx_blk0 = rows.n_lat // tc
    chunk = _pick(t, 256, V7X_SUBLANES)
    conv_chunk = _pick(t, 1024, V7X_SUBLANES)
    vec = lambda rws: pl.BlockSpec((rws, ln), lambda bb, g: (0, g))
    est = 2 * (t + tc) * ln * (2 + 2 + 4) + (t + tc) * ln * 4 + 64 * chunk * ln * 4
    return pl.pallas_call(
        functools.partial(_lru_body, seq=t, ctx=tc, chunk=chunk, conv_chunk=conv_chunk),
        grid=(b, groups),
        in_specs=[
            pl.BlockSpec((t, ln), lambda bb, g: (bb, xb_col + g)),
            pl.BlockSpec((tc, ln), lambda bb, g: (ctx_blk0 + bb, xb_col + g)),
            pl.BlockSpec((t, ln), lambda bb, g: (bb, gb_col + g)),
            pl.BlockSpec((tc, ln), lambda bb, g: (ctx_blk0 + bb, gb_col + g)),
            vec(CONV_W), vec(1),
            pl.BlockSpec((2, 1, ln, ln), lambda bb, g: (0, g, 0, 0)), vec(2),
            pl.BlockSpec((2, 1, ln, ln), lambda bb, g: (0, g, 0, 0)), vec(2),
            vec(2),
        ],
        out_specs=[
            pl.BlockSpec((t, ln), lambda bb, g: (bb, g)),
            pl.BlockSpec((tc, ln), lambda bb, g: (bb, g)),
        ],
        out_shape=[jax.ShapeDtypeStruct((b * t, width), F32),
                   jax.ShapeDtypeStruct((b * tc, width), F32)],
        scratch_shapes=[pltpu.VMEM((t, ln), F32), pltpu.VMEM((tc, ln), F32)],
        compiler_params=_cparams(("arbitrary", "arbitrary"), est),
    )(p, p, p, p, conv_w, conv_b.reshape(1, width), w_a, b_a, w_i, b_i, lam)


def _log_sigmoid(x):
    return jnp.minimum(x, 0.0) - jnp.log(1.0 + jnp.exp(-jnp.abs(x)))


def _linattn_body(*refs, kind, reverse, tb, dk, dv):
    it = iter(refs)
    q_ref, k_ref, v_ref = next(it), next(it), next(it)
    if kind == "ret":
        cos_ref, sin_ref, dec_ref = next(it), next(it), next(it)
    else:
        ga_ref, wg_ref, bg_ref = next(it), next(it), next(it)
    if reverse:
        of_ref, g_ref, gain_ref = next(it), next(it), next(it)
    o_ref, st = next(it), next(it)

    @pl.when(pl.program_id(2) == 0)
    def _():
        st[...] = jnp.zeros(st.shape, F32)

    q = q_ref[...].astype(F32)
    k = k_ref[...].astype(F32)
    if kind == "ret":
        c, s = cos_ref[...], sin_ref[...]
        hd = dk // 2
        q = jnp.concatenate([q[:, :hd] * c - q[:, hd:] * s, q[:, :hd] * s + q[:, hd:] * c], axis=1)
        k = jnp.concatenate([k[:, :hd] * c - k[:, hd:] * s, k[:, :hd] * s + k[:, hd:] * c], axis=1)
        la = jnp.broadcast_to(dec_ref[0], (tb, dk))
    else:
        z = jnp.dot(ga_ref[...], wg_ref[0], preferred_element_type=F32) + bg_ref[0]
        la = _log_sigmoid(z) * (1.0 / GLA_TAU)

    n_ch = tb // CHUNK
    ri = lax.broadcasted_iota(jnp.int32, (CHUNK, CHUNK), 0)
    ci = lax.broadcasted_iota(jnp.int32, (CHUNK, CHUNK), 1)
    mask = (ci >= ri) if reverse else (ci <= ri)
    nt = (((1,), (1,)), ((), ()))
    tn = (((0,), (0,)), ((), ()))
    outs = [None] * n_ch
    for c in (range(n_ch - 1, -1, -1) if reverse else range(n_ch)):
        sl = slice(c * CHUNK, (c + 1) * CHUNK)
        bcum = _cumsum_rows(la[sl], reverse)
        bend = bcum[0:1, :] if reverse else bcum[CHUNK - 1:CHUNK, :]
        qe = (q[sl] * jnp.exp(bcum)).astype(BF16)
        ke = (k[sl] * jnp.exp(-bcum)).astype(BF16)
        ks = (k[sl] * jnp.exp(bend - bcum)).astype(BF16)
        vc = v_ref[sl, :]
        att = lax.dot_general(qe, ke, nt, preferred_element_type=F32)
        att = jnp.where(mask, att, 0.0).astype(BF16)
        o = jnp.dot(att, vc, preferred_element_type=F32)
        o = o + lax.dot_general(qe, st[...].astype(BF16), nt, preferred_element_type=F32)
        st[...] = st[...] * jnp.exp(bend) + lax.dot_general(vc, ks, tn, preferred_element_type=F32)
        outs[c] = o
    o = jnp.concatenate(outs, axis=0)
    if not reverse:
        o_ref[...] = o
    else:
        y = of_ref[...] + o
        ms = jnp.mean(y * y, axis=-1, keepdims=True)
        y = (y * lax.rsqrt(ms + EPS)) * gain_ref[...]
        g = g_ref[...].astype(F32)
        o_ref[...] = (y * (g * jax.nn.sigmoid(g))).astype(o_ref.dtype)


def _linattn(p, cols, heads, kind, reverse, rows, extra, o_fwd=None, gain=None):
    b, t, tc = rows.B, rows.T, rows.TC
    dk, dv = RET_DK, RET_DV
    tb = tc
    assert t % tb == 0 and tb % CHUNK == 0
    nt = 1 + t // tb
    ctx_blk0 = rows.n_lat // tb
    per_seq = t // tb

    def row_blk(bb, tt):
        lat = (per_seq - tt) if reverse else (tt - 1)
        return jnp.where(tt == 0, ctx_blk0 + bb, bb * per_seq + lat)

    qc, kc, vc, gc = cols
    in_specs = [
        pl.BlockSpec((tb, dk), lambda bb, h, tt: (row_blk(bb, tt), qc // dk + h)),
        pl.BlockSpec((tb, dk), lambda bb, h, tt: (row_blk(bb, tt), kc // dk + h)),
        pl.BlockSpec((tb, dv), lambda bb, h, tt: (row_blk(bb, tt), vc // dv + h)),
    ]
    args = [p, p, p]
    if kind == "ret":
        cos, sin, dec = extra
        tab = pl.BlockSpec((tb, dk // 2), lambda bb, h, tt: (row_blk(bb, tt), 0))
        in_specs += [tab, tab, pl.BlockSpec((1, 1, dk), lambda bb, h, tt: (h, 0, 0))]
        args += [cos, sin, dec]
    else:
        ga_col, wg, bg = extra
        in_specs += [
            pl.BlockSpec((tb, V7X_LANES), lambda bb, h, tt: (row_blk(bb, tt), ga_col // V7X_LANES)),
            pl.BlockSpec((1, V7X_LANES, dk), lambda bb, h, tt: (0, 0, h)),
            pl.BlockSpec((1, 1, dk), lambda bb, h, tt: (0, 0, h)),
        ]
        args += [p, wg, bg]
    if reverse:
        in_specs += [
            pl.BlockSpec((tb, dv), lambda bb, h, tt: (row_blk(bb, tt), h)),
            pl.BlockSpec((tb, dv), lambda bb, h, tt: (row_blk(bb, tt), gc // dv + h)),
            pl.BlockSpec((1, dv), lambda bb, h, tt: (0, h)),
        ]
        args += [o_fwd, p, gain]
    est = 2 * tb * (2 * dk + dv) * 2 + 4 * tb * dv * 4 + 3 * dv * dk * 4 + 24 * tb * dk * 4 + 4 * tb * dv * 4
    return pl.pallas_call(
        functools.partial(_linattn_body, kind=kind, reverse=reverse, tb=tb, dk=dk, dv=dv),
        grid=(b, heads, nt),
        in_specs=in_specs,
        out_specs=pl.BlockSpec((tb, dv), lambda bb, h, tt: (row_blk(bb, tt), h)),
        out_shape=jax.ShapeDtypeStruct((rows.R, heads * dv), BF16 if reverse else F32),
        scratch_shapes=[pltpu.VMEM((dv, dk), F32)],
        compiler_params=_cparams(("arbitrary", "arbitrary", "arbitrary"), est),
    )(*args)


def _pad_cols(w, n):
    return jnp.pad(w, ((0, 0), (0, n - w.shape[1])))


def _axial_angles(t, rot_dim):
    n_rows = t // GRID_W
    row = jnp.repeat(jnp.arange(n_rows, dtype=F32), GRID_W)
    col = jnp.tile(jnp.arange(GRID_W, dtype=F32), n_rows)
    n_freq = rot_dim // 4
    inv = ROPE_BASE ** (-jnp.arange(n_freq, dtype=F32) / n_freq)
    return jnp.concatenate([row[:, None] * inv, col[:, None] * inv], axis=-1)


def _row_tables(rows, rot_dim):
    ang = _axial_angles(rows.T, rot_dim)
    half = rot_dim // 2
    cos = jnp.concatenate([jnp.tile(jnp.cos(ang), (rows.B, 1)), jnp.ones((rows.n_ctx, half), F32)], axis=0)
    sin = jnp.concatenate([jnp.tile(jnp.sin(ang), (rows.B, 1)), jnp.zeros((rows.n_ctx, half), F32)], axis=0)
    return cos, sin


def _mla_tables(rows):
    cos, sin = _row_tables(rows, MLA_ROPE)
    half = MLA_ROPE // 2
    z = jnp.zeros((rows.R, V7X_LANES - MLA_ROPE), F32)
    zh = jnp.zeros((rows.R, half), F32)
    c = jnp.concatenate([cos, cos, z], axis=1)
    s1 = jnp.concatenate([-sin, zh, z], axis=1)
    s2 = jnp.concatenate([zh, sin, z], axis=1)
    return c, s1, s2


def _even_mixer(u, h, mods, rows, w_in, q_norm, w_uq, kv_norm, w_ukv, conv_w, conv_b,
                w_a, b_a, w_i, b_i, lam, w_out, tables):
    d = u.shape[1]
    q_rank, kv_rank, lru_w = q_norm.shape[0], kv_norm.shape[0], conv_w.shape[1]
    ln = V7X_LANES
    o_cq, o_ckv, o_kr, o_xb, o_gb = np.cumsum([0, q_rank, kv_rank, MLA_ROPE, lru_w]).tolist()
    w_cols = jnp.concatenate([w_in[:, :o_kr], w_in[:, o_xb:], w_in[:, o_kr:o_xb]], axis=1)
    n_real = w_cols.shape[1] + (ln - MLA_ROPE)
    n_pad = -(-n_real // (5 * V7X_MXU_DIM)) * (5 * V7X_MXU_DIM)
    p = _proj(u, _pad_cols(w_cols, n_pad).astype(BF16), rows, 5 * V7X_MXU_DIM)
    c_cq, c_ckv, c_xb, c_gb, c_kr = 0, q_rank, q_rank + kv_rank, q_rank + kv_rank + lru_w, q_rank + kv_rank + 2 * lru_w

    scale = (MLA_NOPE + MLA_ROPE) ** -0.5
    wq = (w_uq * scale).reshape(q_rank, MLA_HEADS, MLA_NOPE + MLA_ROPE)
    wq = jnp.pad(wq, ((0, 0), (0, 0), (0, 2 * ln - MLA_NOPE - MLA_ROPE))).reshape(q_rank, MLA_HEADS * 2 * ln)
    assert c_cq % q_rank == 0 and c_ckv % kv_rank == 0
    q = _lowrank_up(p, c_cq // q_rank, q_rank, q_norm, wq.astype(BF16), tables, rows, MLA_HEADS, True)
    kv = _lowrank_up(p, c_ckv // kv_rank, kv_rank, kv_norm, w_ukv.astype(BF16), tables, rows, MLA_HEADS, False)
    kr = _kr_rope(p, c_kr // ln, tables, rows)

    att = _attention(q, kv, kr, rows, MLA_HEADS, latent=True)
    att_c = _attention(q, kv, kr, rows, MLA_HEADS, latent=False)
    att = lax.dynamic_update_slice(att, att_c[rows.n_lat:], (rows.n_lat, 0))

    y_l, y_c = _rglru(p, c_xb // ln, c_gb // ln, lru_w, conv_w, conv_b,
                      w_a.astype(BF16), b_a, w_i.astype(BF16), b_i, lam, rows)
    lru = jnp.concatenate([y_l, y_c], axis=0).astype(BF16)
    n_att = MLA_HEADS * MLA_V
    return _res_matmul([att, lru], [w_out[:n_att].astype(BF16), w_out[n_att:].astype(BF16)],
                       h, mods, 1, 1.0, rows)


def _odd_mixer(u, h, mods, rows, w_in, ret_log_decay, ret_norm, gla_w_gate2, gla_b_gate, gla_norm, w_out,
               tables):
    ln = V7X_LANES
    sizes = [RET_HEADS * RET_DK, RET_HEADS * RET_DK, RET_HEADS * RET_DV, RET_HEADS * RET_DV,
             GLA_HEADS * GLA_DK, GLA_HEADS * GLA_DK, GLA_HEADS * GLA_DV, GLA_HEADS * GLA_DV]
    offs = np.cumsum([0] + sizes).tolist()
    rq, rk, rv, rg, gq, gk, gv, gr, ga = offs
    col_scale = jnp.ones((w_in.shape[1],), F32)
    col_scale = col_scale.at[rk:rv].set(RET_DK ** -0.5).at[gq:gk].set(GLA_DK ** -0.5)
    n_real = ga + ln
    n_pad = -(-n_real // (5 * V7X_MXU_DIM)) * (5 * V7X_MXU_DIM)
    p = _proj(u, _pad_cols(w_in * col_scale, n_pad).astype(BF16), rows, 5 * V7X_MXU_DIM)

    cos, sin = tables
    ys = []
    for kind, heads, cols, norm in (("ret", RET_HEADS, (rq, rk, rv, rg), ret_norm),
                                    ("gla", GLA_HEADS, (gq, gk, gv, gr), gla_norm)):
        o_f = None
        for d, reverse in enumerate((False, True)):
            if kind == "ret":
                dec = jnp.broadcast_to(ret_log_decay[d][:, None, None], (heads, 1, RET_DK)).astype(F32)
                extra = (cos, sin, dec)
            else:
                wg = jnp.zeros((1, ln, heads * GLA_DK), F32).at[0, d * GLA_RANK:(d + 1) * GLA_RANK].set(gla_w_gate2[d])
                extra = (ga, wg.astype(BF16), gla_b_gate[d].reshape(1, 1, heads * GLA_DK))
            out = _linattn(p, cols, heads, kind, reverse, rows, extra, o_fwd=o_f,
                           gain=norm.reshape(1, heads * RET_DV))
            if reverse:
                ys.append(out)
            else:
                o_f = out
    n_ret = RET_HEADS * RET_DV
    return _res_matmul(ys, [w_out[:n_ret].astype(BF16), w_out[n_ret:].astype(BF16)], h, mods, 1, 1.0, rows)


def kernel(x, c, ctx, c_ctx, ada_w, ada_b, norm_w, ffn_w_gate, ffn_w_up, ffn_w_down, ev_w_in, mla_q_norm, mla_w_uq, mla_kv_norm, mla_w_ukv, lru_conv_w, lru_conv_b, lru_w_a, lru_b_a, lru_w_i, lru_b_i, lru_lambda, ev_w_out, od_w_in, ret_log_decay, ret_norm, gla_w_gate2, gla_b_gate, gla_norm, od_w_out, final_norm_w):
    batch, seq, d = x.shape
    tc = ctx.shape[1]
    depth = ada_w.shape[0]
    rows = _Rows(batch, seq, tc)
    h = jnp.concatenate([x.reshape(batch * seq, d), ctx.reshape(batch * tc, d)], axis=0)

    n_sets = 1 + batch
    c_rows = jnp.concatenate([c_ctx[None, :], c], axis=0)
    c_rows = jnp.pad(c_rows, ((0, -n_sets % V7X_SUBLANES), (0, 0)))
    mods_all = _ada(c_rows, ada_w, ada_b)[:, :n_sets].reshape(depth, n_sets, 9, d)

    mla_tabs = _mla_tables(rows)
    ret_tabs = _row_tables(rows, RET_DK)

    for l in range(depth):
        mods = mods_all[l]

        def ffn(hh, k, idx):
            u = _norm_mod(hh, norm_w[l, k], mods, k, rows, BF16)
            a = _ffn_up(u, ffn_w_gate[l, idx].astype(BF16), ffn_w_up[l, idx].astype(BF16), rows)
            return _res_matmul([a], [ffn_w_down[l, idx].astype(BF16)], hh, mods, k, FFN_RES, rows)

        h = ffn(h, 0, 0)
        u = _norm_mod(h, norm_w[l, 1], mods, 1, rows, BF16)
        if l % 2 == 0:
            e = l // 2
            h = _even_mixer(u, h, mods, rows, ev_w_in[e], mla_q_norm[e], mla_w_uq[e], mla_kv_norm[e],
                            mla_w_ukv[e], lru_conv_w[e], lru_conv_b[e], lru_w_a[e], lru_b_a[e],
                            lru_w_i[e], lru_b_i[e], lru_lambda[e], ev_w_out[e], mla_tabs)
        else:
            o = l // 2
            h = _odd_mixer(u, h, mods, rows, od_w_in[o], ret_log_decay[o], ret_norm[o], gla_w_gate2[o],
                           gla_b_gate[o], gla_norm[o], od_w_out[o], ret_tabs)
        h = ffn(h, 2, 1)

    out = _norm_mod(h, final_norm_w, mods_all[0], None, rows, F32, n_rows=rows.n_lat)
    return out.reshape(batch, seq, d)
```

```python
import functools
import math

import jax
import jax.numpy as jnp
import numpy as np
from jax import lax
from jax.experimental import pallas as pl
from jax.experimental.pallas import tpu as pltpu

F32 = jnp.float32
BF16 = jnp.bfloat16

V7X_LANES = 128
V7X_SUBLANES = 8
V7X_MXU_DIM = 256
V7X_VMEM_BYTES = 64 * 1024 * 1024
V7X_VMEM_CEILING = 58 * 1024 * 1024

GRID_W = 64
EPS = 1e-6
ROPE_BASE = 10000.0
FFN_RES = 0.5
MLA_HEADS = 16
MLA_NOPE = 128
MLA_ROPE = 64
MLA_V = 128
LRU_BLOCKS = 16
LRU_C = 8.0
CONV_W = 4
CONV_PAD_L = 2
RET_HEADS = 4
RET_DK = 256
RET_DV = 512
GLA_HEADS = 4
GLA_DK = 256
GLA_DV = 512
GLA_RANK = 16
GLA_TAU = 16.0
CHUNK = 64
LINATTN_HEADS_PER_STEP = 2


def _cparams(sem, vmem_est):
    limit = int(min(V7X_VMEM_CEILING, max(vmem_est * 5 // 4 + (4 << 20), 16 << 20)))
    return pltpu.CompilerParams(dimension_semantics=sem, vmem_limit_bytes=limit)


def _pick(n, cap, mult):
    best = None
    for t in range(mult, min(n, cap) + 1, mult):
        if n % t == 0:
            best = t
    assert best is not None, (n, cap, mult)
    return best


class _Rows:
    def __init__(self, batch, seq, ctx):
        self.B, self.T, self.TC = batch, seq, ctx
        self.n_lat = batch * seq
        self.n_ctx = batch * ctx
        self.R = self.n_lat + self.n_ctx

    def tile(self, cap):
        g = math.gcd(self.T, self.n_ctx)
        return _pick(g, cap, V7X_SUBLANES)

    def set_of_tile(self, i, tm):
        n_lat_tiles = self.n_lat // tm
        per_seq = self.T // tm
        return jnp.where(i < n_lat_tiles, 1 + i // per_seq, 0)


def _ada_body(c_ref, w_ref, b_ref, o_ref):
    s = c_ref[...]
    s = (s * jax.nn.sigmoid(s)).astype(BF16)
    w = w_ref[0].astype(BF16)
    o_ref[0] = jnp.dot(s, w, preferred_element_type=F32) + b_ref[0]


def _ada(c_rows, ada_w, ada_b):
    depth, d, n = ada_w.shape
    rows = c_rows.shape[0]
    tn = _pick(n, 512, V7X_LANES)
    est = 2 * d * tn * 4 + d * tn * 2 + 4 * rows * d * 4
    return pl.pallas_call(
        _ada_body,
        grid=(depth, n // tn),
        in_specs=[
            pl.BlockSpec((rows, d), lambda l, j: (0, 0)),
            pl.BlockSpec((1, d, tn), lambda l, j: (l, 0, j)),
            pl.BlockSpec((1, 1, tn), lambda l, j: (l, 0, j)),
        ],
        out_specs=pl.BlockSpec((1, rows, tn), lambda l, j: (l, 0, j)),
        out_shape=jax.ShapeDtypeStruct((depth, rows, n), F32),
        compiler_params=_cparams(("arbitrary", "arbitrary"), est),
    )(c_rows, ada_w, ada_b.reshape(depth, 1, n))


def _norm_body(h_ref, g_ref, m_ref, o_ref, *, k):
    x = h_ref[...].astype(F32)
    ms = jnp.mean(x * x, axis=-1, keepdims=True)
    y = (x * lax.rsqrt(ms + EPS)) * g_ref[...]
    if k is not None:
        y = y * (1.0 + m_ref[0, 3 * k + 1:3 * k + 2, :]) + m_ref[0, 3 * k:3 * k + 1, :]
    o_ref[...] = y.astype(o_ref.dtype)


def _norm_mod(h, gain, mods, k, rows, out_dtype, n_rows=None):
    n_rows = h.shape[0] if n_rows is None else n_rows
    d = h.shape[1]
    tm = rows.tile(256)
    est = 2 * tm * d * (4 + 4) + 2 * tm * d * 4
    return pl.pallas_call(
        functools.partial(_norm_body, k=k),
        grid=(n_rows // tm,),
        in_specs=[
            pl.BlockSpec((tm, d), lambda i: (i, 0)),
            pl.BlockSpec((1, d), lambda i: (0, 0)),
            pl.BlockSpec((1, 9, d), lambda i: (rows.set_of_tile(i, tm), 0, 0)),
        ],
        out_specs=pl.BlockSpec((tm, d), lambda i: (i, 0)),
        out_shape=jax.ShapeDtypeStruct((n_rows, d), out_dtype),
        compiler_params=_cparams(("arbitrary",), est),
    )(h, gain.reshape(1, d), mods)


def _mm_up_body(x_ref, wg_ref, wu_ref, o_ref):
    x = x_ref[...]
    g = jnp.dot(x, wg_ref[...], preferred_element_type=F32)
    u = jnp.dot(x, wu_ref[...], preferred_element_type=F32)
    o_ref[...] = (g * jax.nn.sigmoid(g) * u).astype(o_ref.dtype)


def _w_spec(w, lead, kk, tn, row_block=0):
    assert w.ndim == len(lead) + 2
    return pl.BlockSpec((None,) * len(lead) + (kk, tn), lambda j, i: tuple(lead) + (row_block, j))


def _ffn_up(u, w_gate, w_up, lead, rows):
    r, k = u.shape
    n = w_gate.shape[-1]
    tm = rows.tile(512)
    tn = _pick(n, 512, V7X_LANES)
    est = 2 * tm * k * 2 + 4 * k * tn * 2 + 2 * tm * tn * 2 + 3 * tm * tn * 4
    return pl.pallas_call(
        _mm_up_body,
        grid=(n // tn, r // tm),
        in_specs=[
            pl.BlockSpec((tm, k), lambda j, i: (i, 0)),
            _w_spec(w_gate, lead, k, tn),
            _w_spec(w_up, lead, k, tn),
        ],
        out_specs=pl.BlockSpec((tm, tn), lambda j, i: (i, j)),
        out_shape=jax.ShapeDtypeStruct((r, n), BF16),
        compiler_params=_cparams(("arbitrary", "arbitrary"), est),
    )(u, w_gate, w_up)


def _mm_proj_body(x_ref, w_ref, o_ref):
    o_ref[...] = jnp.dot(x_ref[...], w_ref[...], preferred_element_type=F32).astype(o_ref.dtype)


def _proj(u, w, rows, tn_cap):
    r, k = u.shape
    n = w.shape[1]
    tm = rows.tile(512)
    tn = _pick(n, tn_cap, V7X_LANES)
    est = 2 * tm * k * 2 + 2 * k * tn * 2 + 2 * tm * tn * 2 + tm * tn * 4
    return pl.pallas_call(
        _mm_proj_body,
        grid=(n // tn, r // tm),
        in_specs=[
            pl.BlockSpec((tm, k), lambda j, i: (i, 0)),
            pl.BlockSpec((k, tn), lambda j, i: (0, j)),
        ],
        out_specs=pl.BlockSpec((tm, tn), lambda j, i: (i, j)),
        out_shape=jax.ShapeDtypeStruct((r, n), BF16),
        compiler_params=_cparams(("arbitrary", "arbitrary"), est),
    )(u, w)


def _mm_res_body(*refs, n_pairs, coef, k):
    h_ref, m_ref, o_ref = refs[2 * n_pairs:]
    acc = None
    for p in range(n_pairs):
        part = jnp.dot(refs[2 * p][...], refs[2 * p + 1][...], preferred_element_type=F32)
        acc = part if acc is None else acc + part
    gate = m_ref[0, 3 * k + 2:3 * k + 3, :]
    o_ref[...] = h_ref[...] + (coef * gate) * acc


def _res_matmul(xs, w, lead, h, mods, k, coef, rows):
    r, n = h.shape
    tm = rows.tile(512)
    tn = _pick(n, 512, V7X_LANES)
    in_specs, args, est = [], [], 0
    for p, x in enumerate(xs):
        kk = x.shape[1]
        assert kk * len(xs) == w.shape[-2]
        in_specs += [pl.BlockSpec((tm, kk), lambda j, i: (i, 0)), _w_spec(w, lead, kk, tn, p)]
        args += [x, w]
        est += 2 * tm * kk * 2 + 2 * kk * tn * 2
    in_specs += [pl.BlockSpec((tm, tn), lambda j, i: (i, j)),
                 pl.BlockSpec((1, 9, tn), lambda j, i: (rows.set_of_tile(i, tm), 0, j))]
    est += 6 * tm * tn * 4
    return pl.pallas_call(
        functools.partial(_mm_res_body, n_pairs=len(xs), coef=coef, k=k),
        grid=(n // tn, r // tm),
        in_specs=in_specs,
        out_specs=pl.BlockSpec((tm, tn), lambda j, i: (i, j)),
        out_shape=jax.ShapeDtypeStruct((r, n), F32),
        compiler_params=_cparams(("arbitrary", "arbitrary"), est),
    )(*args, h, mods)


def _rope_pairs(t, c, s1, s2):
    half = MLA_ROPE // 2
    return t * c + pltpu.roll(t, V7X_LANES - half, 1) * s1 + pltpu.roll(t, half, 1) * s2


def _lowrank_body(x_ref, g_ref, w_ref, c_ref, s1_ref, s2_ref, o_ref, *, heads, rope):
    x = x_ref[...].astype(F32)
    ms = jnp.mean(x * x, axis=-1, keepdims=True)
    xn = ((x * lax.rsqrt(ms + EPS)) * g_ref[...]).astype(BF16)
    hw = 2 * V7X_LANES
    for h in range(heads):
        acc = jnp.dot(xn, w_ref[:, h * hw:(h + 1) * hw], preferred_element_type=F32)
        if rope:
            o_ref[:, h * hw:h * hw + V7X_LANES] = acc[:, :V7X_LANES].astype(o_ref.dtype)
            t = _rope_pairs(acc[:, V7X_LANES:], c_ref[...], s1_ref[...], s2_ref[...])
            o_ref[:, h * hw + V7X_LANES:(h + 1) * hw] = t.astype(o_ref.dtype)
        else:
            o_ref[:, h * hw:(h + 1) * hw] = acc.astype(o_ref.dtype)


def _lowrank_up(p, col_block, k, gain, w, tables, rows, heads, rope):
    r = p.shape[0]
    n = w.shape[1]
    tm = rows.tile(512)
    est = 2 * tm * k * 2 + 2 * k * n * 2 + 2 * tm * n * 2 + 6 * tm * V7X_LANES * 4 + tm * k * 8 + 4 * tm * 256 * 4
    tab_spec = pl.BlockSpec((tm, V7X_LANES), lambda i: (i, 0))
    return pl.pallas_call(
        functools.partial(_lowrank_body, heads=heads, rope=rope),
        grid=(r // tm,),
        in_specs=[
            pl.BlockSpec((tm, k), lambda i: (i, col_block)),
            pl.BlockSpec((1, k), lambda i: (0, 0)),
            pl.BlockSpec((k, n), lambda i: (0, 0)),
            tab_spec, tab_spec, tab_spec,
        ],
        out_specs=pl.BlockSpec((tm, n), lambda i: (i, 0)),
        out_shape=jax.ShapeDtypeStruct((r, n), BF16),
        compiler_params=_cparams(("arbitrary",), est),
    )(p, gain.reshape(1, k), w, *tables)


def _kr_rope_body(x_ref, c_ref, s1_ref, s2_ref, o_ref):
    t = _rope_pairs(x_ref[...].astype(F32), c_ref[...], s1_ref[...], s2_ref[...])
    o_ref[...] = t.astype(o_ref.dtype)


def _kr_rope(p, col_block, tables, rows):
    r = p.shape[0]
    tm = rows.tile(512)
    spec = pl.BlockSpec((tm, V7X_LANES), lambda i: (i, 0))
    return pl.pallas_call(
        _kr_rope_body,
        grid=(r // tm,),
        in_specs=[pl.BlockSpec((tm, V7X_LANES), lambda i: (i, col_block)), spec, spec, spec],
        out_specs=spec,
        out_shape=jax.ShapeDtypeStruct((r, V7X_LANES), BF16),
        compiler_params=_cparams(("arbitrary",), 16 * tm * V7X_LANES * 4),
    )(p, *tables)


ATTN_VT_ROWS = V7X_LANES + 16


def _attn_body(*refs, n_ctx, n_lat, tk):
    if n_lat:
        (q_ref, kc_ref, vc_ref, krc_ref, kl_ref, vl_ref, krl_ref,
         o_ref, kcat, vt_c, vt_l, acc_scr, s_a, s_b) = refs
    else:
        q_ref, kc_ref, vc_ref, krc_ref, o_ref, kcat, vt_c, acc_scr, s_a = refs
    ln = V7X_LANES
    nt = (((1,), (1,)), ((), ()))

    @pl.when(pl.program_id(2) == 0)
    def _():
        kcat[0:n_ctx, 0:ln] = kc_ref[...]
        kcat[0:n_ctx, ln:2 * ln] = krc_ref[...]
        eye = (lax.broadcasted_iota(jnp.int32, (ln, ln), 0)
               == lax.broadcasted_iota(jnp.int32, (ln, ln), 1)).astype(BF16)

        def ones_rows(n):
            first = lax.broadcasted_iota(jnp.int32, (ATTN_VT_ROWS - ln, n), 0) == 0
            return jnp.where(first, 1.0, 0.0).astype(BF16)

        vt_c[0:ln, :] = lax.dot_general(eye, vc_ref[...], nt, preferred_element_type=F32).astype(BF16)
        vt_c[ln:, :] = ones_rows(n_ctx)
        if n_lat:
            kcat[n_ctx:n_ctx + n_lat, 0:ln] = kl_ref[...]
            kcat[n_ctx:n_ctx + n_lat, ln:2 * ln] = krl_ref[...]
            for c in range(n_lat // tk):
                vt_l[c, 0:ln, :] = lax.dot_general(eye, vl_ref[c * tk:(c + 1) * tk, :], nt,
                                                   preferred_element_type=F32).astype(BF16)
                vt_l[c, ln:, :] = ones_rows(tk)

    q = q_ref[...]
    tq = q.shape[0]
    acc_scr[...] = jnp.zeros(acc_scr.shape, F32)

    def scores(k, dst):
        dst[0:k.shape[0], :] = lax.dot_general(k, q, nt, preferred_element_type=F32)

    def softmax_pv(src, n, v_t, m_prev):
        s = src[0:n, :]
        m_next = jnp.maximum(m_prev, jnp.max(s, axis=0, keepdims=True))
        p = jnp.exp2(s - m_next).astype(BF16)
        alpha = jnp.exp2(m_prev - m_next)
        acc_scr[...] = alpha * acc_scr[...] + jnp.dot(v_t, p, preferred_element_type=F32)
        return m_next

    def lat_keys(c):
        return kcat[pl.ds(pl.multiple_of(n_ctx + c * tk, ln), tk), :]

    m = jnp.full((1, tq), -jnp.inf, F32)
    scores(kcat[0:n_ctx, :], s_a)
    if not n_lat:
        m = softmax_pv(s_a, n_ctx, vt_c[...], m)
    else:
        n_ch = n_lat // tk
        scores(lat_keys(0), s_b)
        m = softmax_pv(s_a, n_ctx, vt_c[...], m)

        def pair(i, m):
            c = 2 * i
            scores(lat_keys(c + 1), s_a)
            m = softmax_pv(s_b, tk, vt_l[c], m)
            scores(lat_keys(c + 2), s_b)
            return softmax_pv(s_a, tk, vt_l[c + 1], m)

        m = lax.fori_loop(0, n_ch // 2 - 1, pair, m)
        scores(lat_keys(n_ch - 1), s_a)
        m = softmax_pv(s_b, tk, vt_l[n_ch - 2], m)
        m = softmax_pv(s_a, tk, vt_l[n_ch - 1], m)
    out = acc_scr[0:ln, :] / acc_scr[ln:ln + 1, :]
    o_ref[...] = out.T.astype(o_ref.dtype)


def _attention(q, kv, kr, rows, heads, latent):
    b, t, tc = rows.B, rows.T, rows.TC
    ln = V7X_LANES
    ctx_blk0 = rows.n_lat // tc
    if latent:
        tq = _pick(t, 512, ln)
        nq = t // tq
        tk = _pick(t // 2, 1024, ln)
        assert tk >= tc
        n_lat = t
    else:
        tq, nq, tk, n_lat = tc, 1, tc, 0
    q_row = (lambda bb, qi: bb * nq + qi) if latent else (lambda bb, qi: ctx_blk0 * (tc // tq) + bb)
    in_specs = [
        pl.BlockSpec((tq, 2 * ln), lambda bb, h, qi: (q_row(bb, qi), h)),
        pl.BlockSpec((tc, ln), lambda bb, h, qi: (ctx_blk0 + bb, 2 * h)),
        pl.BlockSpec((tc, ln), lambda bb, h, qi: (ctx_blk0 + bb, 2 * h + 1)),
        pl.BlockSpec((tc, ln), lambda bb, h, qi: (ctx_blk0 + bb, 0)),
    ]
    args = [q, kv, kv, kr]
    scratch = [pltpu.VMEM((tc + n_lat, 2 * ln), BF16), pltpu.VMEM((ATTN_VT_ROWS, tc), BF16)]
    if latent:
        in_specs += [
            pl.BlockSpec((t, ln), lambda bb, h, qi: (bb, 2 * h)),
            pl.BlockSpec((t, ln), lambda bb, h, qi: (bb, 2 * h + 1)),
            pl.BlockSpec((t, ln), lambda bb, h, qi: (bb, 0)),
        ]
        args += [kv, kv, kr]
        scratch.append(pltpu.VMEM((n_lat // tk, ATTN_VT_ROWS, tk), BF16))
    scratch.append(pltpu.VMEM((ATTN_VT_ROWS, tq), F32))
    scratch += [pltpu.VMEM((tk, tq), F32)] * (2 if latent else 1)
    nk = tc + n_lat
    est = (2 * tq * 2 * ln * 2 + 6 * tc * ln * 2 + 6 * n_lat * ln * 2 + nk * 3 * ln * 2
           + 2 * tq * ln * 2 + 2 * tq * ln * 4 + 5 * tq * tk * 4)
    return pl.pallas_call(
        functools.partial(_attn_body, n_ctx=tc, n_lat=n_lat, tk=tk),
        grid=(b, heads, nq),
        in_specs=in_specs,
        out_specs=pl.BlockSpec((tq, ln), lambda bb, h, qi: (q_row(bb, qi), h)),
        out_shape=jax.ShapeDtypeStruct((rows.R, heads * ln), BF16),
        scratch_shapes=scratch,
        compiler_params=_cparams(("arbitrary", "arbitrary", "arbitrary"), est),
    )(*args)


def _scan_rows(a, bv, reverse):
    n = a.shape[0]
    row = lax.broadcasted_iota(jnp.int32, a.shape, 0)
    s = 1
    while s < n:
        if reverse:
            keep = row < (n - s)
            a_sh = jnp.where(keep, pltpu.roll(a, n - s, 0), 1.0)
            b_sh = jnp.where(keep, pltpu.roll(bv, n - s, 0), 0.0)
        else:
            keep = row >= s
            a_sh = jnp.where(keep, pltpu.roll(a, s, 0), 1.0)
            b_sh = jnp.where(keep, pltpu.roll(bv, s, 0), 0.0)
        bv = a * b_sh + bv
        a = a * a_sh
        s *= 2
    return a, bv


def _cumsum_rows(x, reverse):
    n = x.shape[0]
    row = lax.broadcasted_iota(jnp.int32, x.shape, 0)
    s = 1
    while s < n:
        if reverse:
            x = x + jnp.where(row < (n - s), pltpu.roll(x, n - s, 0), 0.0)
        else:
            x = x + jnp.where(row >= s, pltpu.roll(x, s, 0), 0.0)
        s *= 2
    return x


def _gelu_tanh(x):
    return 0.5 * x * (1.0 + jnp.tanh(math.sqrt(2.0 / math.pi) * (x + 0.044715 * (x * x * x))))


def _lru_body(xl_ref, xc_ref, gl_ref, gc_ref, cw_ref, cb_ref, wa_ref, ba_ref, wi_ref, bi_ref, lam_ref,
              yl_ref, yc_ref, cv_l, cv_c, *, seq, ctx, chunk, conv_chunk):
    pad = V7X_SUBLANES
    zeros8 = jnp.zeros((pad, V7X_LANES), F32)

    def conv_into(src_ref, dst, n, step):
        for c0 in range(0, n, step):
            lo = zeros8 if c0 == 0 else src_ref[c0 - pad:c0, :].astype(F32)
            hi = zeros8 if c0 + step >= n else src_ref[c0 + step:c0 + step + pad, :].astype(F32)
            ext = jnp.concatenate([lo, src_ref[c0:c0 + step, :].astype(F32), hi], axis=0)
            out = cb_ref[...]
            for kk in range(CONV_W):
                off = pad + kk - CONV_PAD_L
                out = out + ext[off:off + step, :] * cw_ref[kk:kk + 1, :]
            dst[c0:c0 + step, :] = out

    conv_into(xl_ref, cv_l, seq, conv_chunk)
    conv_into(xc_ref, cv_c, ctx, min(ctx, conv_chunk))

    for d, reverse in enumerate((False, True)):
        lam = lam_ref[d:d + 1, :]
        neg_sp = -LRU_C * (jnp.maximum(-lam, 0.0) + jnp.log(1.0 + jnp.exp(-jnp.abs(lam))))
        wa = wa_ref[d, 0]
        wi = wi_ref[d, 0]
        ba = ba_ref[d:d + 1, :]
        bi = bi_ref[d:d + 1, :]

        def block(x, carry):
            xg = x.astype(BF16)
            r = jax.nn.sigmoid(jnp.dot(xg, wa, preferred_element_type=F32) + ba)
            i = jax.nn.sigmoid(jnp.dot(xg, wi, preferred_element_type=F32) + bi)
            a = jnp.exp(neg_sp * r)
            bv = jnp.sqrt(1.0 - a * a) * (i * x)
            a_cum, h = _scan_rows(a, bv, reverse)
            h = a_cum * carry + h
            n = x.shape[0]
            new_carry = h[0:1, :] if reverse else h[n - 1:n, :]
            return h, new_carry

        def emit(dst_ref, g_ref, start, n, h):
            if not reverse:
                dst_ref[pl.ds(start, n), :] = h.astype(dst_ref.dtype)
            else:
                tot = dst_ref[pl.ds(start, n), :].astype(F32) + h
                g = g_ref[pl.ds(start, n), :].astype(F32)
                dst_ref[pl.ds(start, n), :] = (tot * _gelu_tanh(g)).astype(dst_ref.dtype)

        carry = jnp.zeros((1, V7X_LANES), F32)
        cchunk = min(ctx, chunk)
        n_cc = ctx // cchunk
        order = range(n_cc - 1, -1, -1) if reverse else range(n_cc)
        lat_init = None
        for c in order:
            h, carry = block(cv_c[c * cchunk:(c + 1) * cchunk, :], carry)
            if lat_init is None:
                lat_init = h[cchunk - 1:cchunk, :] if reverse else h[0:1, :]
            emit(yc_ref, gc_ref, c * cchunk, cchunk, h)
        carry = lat_init

        n_lc = seq // chunk

        def body(it, carry):
            c = (n_lc - 1 - it) if reverse else it
            start = pl.multiple_of(c * chunk, chunk)
            h, carry = block(cv_l[pl.ds(start, chunk), :], carry)
            emit(yl_ref, gl_ref, start, chunk, h)
            return carry

        lax.fori_loop(0, n_lc, body, carry)


def _rglru(p, xb_col, gb_col, width, conv_w, conv_b, w_a, b_a, w_i, b_i, lam, rows):
    b, t, tc = rows.B, rows.T, rows.TC
    ln = V7X_LANES
    groups = width // ln
    ctx_blk0 = rows.n_lat // tc
    chunk = _pick(t, 256, V7X_SUBLANES)
    conv_chunk = _pick(t, 1024, V7X_SUBLANES)
    vec = lambda rws: pl.BlockSpec((rws, ln), lambda bb, g: (0, g))
    est = 2 * (t + tc) * ln * (2 + 2 + 4) + (t + tc) * ln * 4 + 64 * chunk * ln * 4
    return pl.pallas_call(
        functools.partial(_lru_body, seq=t, ctx=tc, chunk=chunk, conv_chunk=conv_chunk),
        grid=(b, groups),
        in_specs=[
            pl.BlockSpec((t, ln), lambda bb, g: (bb, xb_col + g)),
            pl.BlockSpec((tc, ln), lambda bb, g: (ctx_blk0 + bb, xb_col + g)),
            pl.BlockSpec((t, ln), lambda bb, g: (bb, gb_col + g)),
            pl.BlockSpec((tc, ln), lambda bb, g: (ctx_blk0 + bb, gb_col + g)),
            vec(CONV_W), vec(1),
            pl.BlockSpec((2, 1, ln, ln), lambda bb, g: (0, g, 0, 0)), vec(2),
            pl.BlockSpec((2, 1, ln, ln), lambda bb, g: (0, g, 0, 0)), vec(2),
            vec(2),
        ],
        out_specs=[
            pl.BlockSpec((t, ln), lambda bb, g: (bb, g)),
            pl.BlockSpec((tc, ln), lambda bb, g: (bb, g)),
        ],
        out_shape=[jax.ShapeDtypeStruct((b * t, width), F32),
                   jax.ShapeDtypeStruct((b * tc, width), F32)],
        scratch_shapes=[pltpu.VMEM((t, ln), F32), pltpu.VMEM((tc, ln), F32)],
        compiler_params=_cparams(("arbitrary", "arbitrary"), est),
    )(p, p, p, p, conv_w, conv_b.reshape(1, width), w_a, b_a, w_i, b_i, lam)


def _log_sigmoid(x):
    return jnp.minimum(x, 0.0) - jnp.log(1.0 + jnp.exp(-jnp.abs(x)))


def _linattn_body(*refs, kind, reverse, tb, dk, dv):
    it = iter(refs)
    q_ref, k_ref, v_ref = next(it), next(it), next(it)
    if kind == "ret":
        cos_ref, sin_ref, dec_ref = next(it), next(it), next(it)
    else:
        ga_ref, wg_ref, bg_ref = next(it), next(it), next(it)
    if reverse:
        of_ref, g_ref, gain_ref = next(it), next(it), next(it)
    o_ref, st = next(it), next(it)
    hp = st.shape[0]

    @pl.when(pl.program_id(2) == 0)
    def _():
        st[...] = jnp.zeros(st.shape, F32)

    n_ch = tb // CHUNK
    ri = lax.broadcasted_iota(jnp.int32, (CHUNK, CHUNK), 0)
    ci = lax.broadcasted_iota(jnp.int32, (CHUNK, CHUNK), 1)
    mask = (ci >= ri) if reverse else (ci <= ri)
    nt = (((1,), (1,)), ((), ()))
    tn = (((0,), (0,)), ((), ()))
    pos = lax.broadcasted_iota(jnp.int32, (tb, 1), 0) % CHUNK
    n_terms = ((CHUNK - pos) if reverse else (pos + 1)).astype(F32)
    for hh in range(hp):
        q = q_ref[:, hh * dk:(hh + 1) * dk].astype(F32)
        k = k_ref[:, hh * dk:(hh + 1) * dk].astype(F32)
        if kind == "ret":
            c, s = cos_ref[...], sin_ref[...]
            hd = dk // 2
            q = jnp.concatenate([q[:, :hd] * c - q[:, hd:] * s, q[:, :hd] * s + q[:, hd:] * c], axis=1)
            k = jnp.concatenate([k[:, :hd] * c - k[:, hd:] * s, k[:, :hd] * s + k[:, hd:] * c], axis=1)
            bcum_all = n_terms * dec_ref[hh]
        else:
            z = jnp.dot(ga_ref[...], wg_ref[0, :, hh * dk:(hh + 1) * dk], preferred_element_type=F32)
            la = _log_sigmoid(z + bg_ref[0, :, hh * dk:(hh + 1) * dk]) * (1.0 / GLA_TAU)
        outs = [None] * n_ch
        for c in (range(n_ch - 1, -1, -1) if reverse else range(n_ch)):
            sl = slice(c * CHUNK, (c + 1) * CHUNK)
            bcum = bcum_all[sl] if kind == "ret" else _cumsum_rows(la[sl], reverse)
            bend = bcum[0:1, :] if reverse else bcum[CHUNK - 1:CHUNK, :]
            qe = (q[sl] * jnp.exp(bcum)).astype(BF16)
            ke = (k[sl] * jnp.exp(-bcum)).astype(BF16)
            ks = (k[sl] * jnp.exp(bend - bcum)).astype(BF16)
            vc = v_ref[sl, hh * dv:(hh + 1) * dv]
            att = lax.dot_general(qe, ke, nt, preferred_element_type=F32)
            att = jnp.where(mask, att, 0.0).astype(BF16)
            o = jnp.dot(att, vc, preferred_element_type=F32)
            o = o + lax.dot_general(qe, st[hh].astype(BF16), nt, preferred_element_type=F32)
            st[hh] = st[hh] * jnp.exp(bend) + lax.dot_general(vc, ks, tn, preferred_element_type=F32)
            outs[c] = o
        o = jnp.concatenate(outs, axis=0)
        cs = slice(hh * dv, (hh + 1) * dv)
        if not reverse:
            o_ref[:, cs] = o
        else:
            y = of_ref[:, cs] + o
            ms = jnp.mean(y * y, axis=-1, keepdims=True)
            y = (y * lax.rsqrt(ms + EPS)) * gain_ref[:, cs]
            g = g_ref[:, cs].astype(F32)
            o_ref[:, cs] = (y * (g * jax.nn.sigmoid(g))).astype(o_ref.dtype)


def _linattn(p, cols, heads, kind, reverse, rows, extra, o_fwd=None, gain=None):
    b, t, tc = rows.B, rows.T, rows.TC
    dk, dv = RET_DK, RET_DV
    tb = tc
    assert t % tb == 0 and tb % CHUNK == 0
    nt = 1 + t // tb
    ctx_blk0 = rows.n_lat // tb
    per_seq = t // tb

    def row_blk(bb, tt):
        lat = (per_seq - tt) if reverse else (tt - 1)
        return jnp.where(tt == 0, ctx_blk0 + bb, bb * per_seq + lat)

    qc, kc, vc, gc = cols
    hp = math.gcd(heads, LINATTN_HEADS_PER_STEP)
    wk, wv = hp * dk, hp * dv
    assert all(o % wk == 0 for o in (qc, kc)) and all(o % wv == 0 for o in (vc, gc))
    in_specs = [
        pl.BlockSpec((tb, wk), lambda bb, h, tt: (row_blk(bb, tt), qc // wk + h)),
        pl.BlockSpec((tb, wk), lambda bb, h, tt: (row_blk(bb, tt), kc // wk + h)),
        pl.BlockSpec((tb, wv), lambda bb, h, tt: (row_blk(bb, tt), vc // wv + h)),
    ]
    args = [p, p, p]
    if kind == "ret":
        cos, sin, dec = extra
        tab = pl.BlockSpec((tb, dk // 2), lambda bb, h, tt: (row_blk(bb, tt), 0))
        in_specs += [tab, tab, pl.BlockSpec((hp, 1, dk), lambda bb, h, tt: (h, 0, 0))]
        args += [cos, sin, dec]
    else:
        ga_col, wg, bg = extra
        in_specs += [
            pl.BlockSpec((tb, V7X_LANES), lambda bb, h, tt: (row_blk(bb, tt), ga_col // V7X_LANES)),
            pl.BlockSpec((1, V7X_LANES, wk), lambda bb, h, tt: (0, 0, h)),
            pl.BlockSpec((1, 1, wk), lambda bb, h, tt: (0, 0, h)),
        ]
        args += [p, wg, bg]
    if reverse:
        in_specs += [
            pl.BlockSpec((tb, wv), lambda bb, h, tt: (row_blk(bb, tt), h)),
            pl.BlockSpec((tb, wv), lambda bb, h, tt: (row_blk(bb, tt), gc // wv + h)),
            pl.BlockSpec((1, wv), lambda bb, h, tt: (0, h)),
        ]
        args += [o_fwd, p, gain]
    est = hp * (2 * tb * (2 * dk + dv) * 2 + 6 * tb * dv * 4 + 3 * dv * dk * 4 + 24 * tb * dk * 4 + 4 * tb * dv * 4)
    return pl.pallas_call(
        functools.partial(_linattn_body, kind=kind, reverse=reverse, tb=tb, dk=dk, dv=dv),
        grid=(b, heads // hp, nt),
        in_specs=in_specs,
        out_specs=pl.BlockSpec((tb, wv), lambda bb, h, tt: (row_blk(bb, tt), h)),
        out_shape=jax.ShapeDtypeStruct((rows.R, heads * dv), BF16 if reverse else F32),
        scratch_shapes=[pltpu.VMEM((hp, dv, dk), F32)],
        compiler_params=_cparams(("arbitrary", "arbitrary", "arbitrary"), est),
    )(*args)


def _pad_cols(w, n):
    return jnp.pad(w, ((0, 0), (0, n - w.shape[1])))


def _axial_angles(t, rot_dim):
    n_rows = t // GRID_W
    row = jnp.repeat(jnp.arange(n_rows, dtype=F32), GRID_W)
    col = jnp.tile(jnp.arange(GRID_W, dtype=F32), n_rows)
    n_freq = rot_dim // 4
    inv = ROPE_BASE ** (-jnp.arange(n_freq, dtype=F32) / n_freq)
    return jnp.concatenate([row[:, None] * inv, col[:, None] * inv], axis=-1)


def _row_tables(rows, rot_dim):
    ang = _axial_angles(rows.T, rot_dim)
    half = rot_dim // 2
    cos = jnp.concatenate([jnp.tile(jnp.cos(ang), (rows.B, 1)), jnp.ones((rows.n_ctx, half), F32)], axis=0)
    sin = jnp.concatenate([jnp.tile(jnp.sin(ang), (rows.B, 1)), jnp.zeros((rows.n_ctx, half), F32)], axis=0)
    return cos, sin


def _mla_tables(rows):
    cos, sin = _row_tables(rows, MLA_ROPE)
    half = MLA_ROPE // 2
    z = jnp.zeros((rows.R, V7X_LANES - MLA_ROPE), F32)
    zh = jnp.zeros((rows.R, half), F32)
    c = jnp.concatenate([cos, cos, z], axis=1)
    s1 = jnp.concatenate([-sin, zh, z], axis=1)
    s2 = jnp.concatenate([zh, sin, z], axis=1)
    return c, s1, s2


def _even_mixer(u, h, mods, rows, w_in, q_norm, w_uq, kv_norm, w_ukv, conv_w, conv_b,
                w_a, b_a, w_i, b_i, lam, w_out, tables):
    d = u.shape[1]
    q_rank, kv_rank, lru_w = q_norm.shape[0], kv_norm.shape[0], conv_w.shape[1]
    ln = V7X_LANES
    o_cq, o_ckv, o_kr, o_xb, o_gb = np.cumsum([0, q_rank, kv_rank, MLA_ROPE, lru_w]).tolist()
    w_cols = jnp.concatenate([w_in[:, :o_kr], w_in[:, o_xb:], w_in[:, o_kr:o_xb]], axis=1)
    n_real = w_cols.shape[1] + (ln - MLA_ROPE)
    n_pad = -(-n_real // (5 * V7X_MXU_DIM)) * (5 * V7X_MXU_DIM)
    p = _proj(u, _pad_cols(w_cols, n_pad).astype(BF16), rows, 5 * V7X_MXU_DIM)
    c_cq, c_ckv, c_xb, c_gb, c_kr = 0, q_rank, q_rank + kv_rank, q_rank + kv_rank + lru_w, q_rank + kv_rank + 2 * lru_w

    scale = (MLA_NOPE + MLA_ROPE) ** -0.5 * math.log2(math.e)
    wq = (w_uq * scale).reshape(q_rank, MLA_HEADS, MLA_NOPE + MLA_ROPE)
    wq = jnp.pad(wq, ((0, 0), (0, 0), (0, 2 * ln - MLA_NOPE - MLA_ROPE))).reshape(q_rank, MLA_HEADS * 2 * ln)
    assert c_cq % q_rank == 0 and c_ckv % kv_rank == 0
    q = _lowrank_up(p, c_cq // q_rank, q_rank, q_norm, wq.astype(BF16), tables, rows, MLA_HEADS, True)
    kv = _lowrank_up(p, c_ckv // kv_rank, kv_rank, kv_norm, w_ukv.astype(BF16), tables, rows, MLA_HEADS, False)
    kr = _kr_rope(p, c_kr // ln, tables, rows)

    att = _attention(q, kv, kr, rows, MLA_HEADS, latent=True)
    att_c = _attention(q, kv, kr, rows, MLA_HEADS, latent=False)
    att = lax.dynamic_update_slice(att, att_c[rows.n_lat:], (rows.n_lat, 0))

    y_l, y_c = _rglru(p, c_xb // ln, c_gb // ln, lru_w, conv_w, conv_b,
                      w_a.astype(BF16), b_a, w_i.astype(BF16), b_i, lam, rows)
    lru = jnp.concatenate([y_l, y_c], axis=0).astype(BF16)
    n_att = MLA_HEADS * MLA_V
    assert MLA_HEADS * MLA_V == lru_w
    return _res_matmul([att, lru], w_out.astype(BF16), (), h, mods, 1, 1.0, rows)


def _odd_mixer(u, h, mods, rows, w_in, ret_log_decay, ret_norm, gla_w_gate2, gla_b_gate, gla_norm, w_out,
               tables):
    ln = V7X_LANES
    sizes = [RET_HEADS * RET_DK, RET_HEADS * RET_DK, RET_HEADS * RET_DV, RET_HEADS * RET_DV,
             GLA_HEADS * GLA_DK, GLA_HEADS * GLA_DK, GLA_HEADS * GLA_DV, GLA_HEADS * GLA_DV]
    offs = np.cumsum([0] + sizes).tolist()
    rq, rk, rv, rg, gq, gk, gv, gr, ga = offs
    col_scale = jnp.ones((w_in.shape[1],), F32)
    col_scale = col_scale.at[rk:rv].set(RET_DK ** -0.5).at[gq:gk].set(GLA_DK ** -0.5)
    n_real = ga + ln
    n_pad = -(-n_real // (5 * V7X_MXU_DIM)) * (5 * V7X_MXU_DIM)
    p = _proj(u, _pad_cols(w_in * col_scale, n_pad).astype(BF16), rows, 5 * V7X_MXU_DIM)

    cos, sin = tables
    ys = []
    for kind, heads, cols, norm in (("ret", RET_HEADS, (rq, rk, rv, rg), ret_norm),
                                    ("gla", GLA_HEADS, (gq, gk, gv, gr), gla_norm)):
        o_f = None
        for d, reverse in enumerate((False, True)):
            if kind == "ret":
                dec = jnp.broadcast_to(ret_log_decay[d][:, None, None], (heads, 1, RET_DK)).astype(F32)
                extra = (cos, sin, dec)
            else:
                wg = jnp.zeros((1, ln, heads * GLA_DK), F32).at[0, d * GLA_RANK:(d + 1) * GLA_RANK].set(gla_w_gate2[d])
                extra = (ga, wg.astype(BF16), gla_b_gate[d].reshape(1, 1, heads * GLA_DK))
            out = _linattn(p, cols, heads, kind, reverse, rows, extra, o_fwd=o_f,
                           gain=norm.reshape(1, heads * RET_DV))
            if reverse:
                ys.append(out)
            else:
                o_f = out
    n_ret = RET_HEADS * RET_DV
    assert RET_HEADS * RET_DV == GLA_HEADS * GLA_DV
    return _res_matmul(ys, w_out.astype(BF16), (), h, mods, 1, 1.0, rows)


def kernel(x, c, ctx, c_ctx, ada_w, ada_b, norm_w, ffn_w_gate, ffn_w_up, ffn_w_down, ev_w_in, mla_q_norm, mla_w_uq, mla_kv_norm, mla_w_ukv, lru_conv_w, lru_conv_b, lru_w_a, lru_b_a, lru_w_i, lru_b_i, lru_lambda, ev_w_out, od_w_in, ret_log_decay, ret_norm, gla_w_gate2, gla_b_gate, gla_norm, od_w_out, final_norm_w):
    batch, seq, d = x.shape
    tc = ctx.shape[1]
    depth = ada_w.shape[0]
    rows = _Rows(batch, seq, tc)
    h = jnp.concatenate([x.reshape(batch * seq, d), ctx.reshape(batch * tc, d)], axis=0)

    n_sets = 1 + batch
    c_rows = jnp.concatenate([c_ctx[None, :], c], axis=0)
    c_rows = jnp.pad(c_rows, ((0, -n_sets % V7X_SUBLANES), (0, 0)))
    mods_all = _ada(c_rows, ada_w, ada_b)[:, :n_sets].reshape(depth, n_sets, 9, d)

    mla_tabs = _mla_tables(rows)
    ret_tabs = _row_tables(rows, RET_DK)
    w_gate_bf, w_up_bf, w_down_bf = (w.astype(BF16) for w in (ffn_w_gate, ffn_w_up, ffn_w_down))

    for l in range(depth):
        mods = mods_all[l]

        def ffn(hh, k, idx):
            u = _norm_mod(hh, norm_w[l, k], mods, k, rows, BF16)
            a = _ffn_up(u, w_gate_bf, w_up_bf, (l, idx), rows)
            return _res_matmul([a], w_down_bf, (l, idx), hh, mods, k, FFN_RES, rows)

        h = ffn(h, 0, 0)
        u = _norm_mod(h, norm_w[l, 1], mods, 1, rows, BF16)
        if l % 2 == 0:
            e = l // 2
            h = _even_mixer(u, h, mods, rows, ev_w_in[e], mla_q_norm[e], mla_w_uq[e], mla_kv_norm[e],
                            mla_w_ukv[e], lru_conv_w[e], lru_conv_b[e], lru_w_a[e], lru_b_a[e],
                            lru_w_i[e], lru_b_i[e], lru_lambda[e], ev_w_out[e], mla_tabs)
        else:
            o = l // 2
            h = _odd_mixer(u, h, mods, rows, od_w_in[o], ret_log_decay[o], ret_norm[o], gla_w_gate2[o],
                           gla_b_gate[o], gla_norm[o], od_w_out[o], ret_tabs)
        h = ffn(h, 2, 1)

    out = _norm_mod(h, final_norm_w, mods_all[0], None, rows, F32, n_rows=rows.n_lat)
    return out.reshape(batch, seq, d)
```

```python
import functools
import math

import jax
import jax.numpy as jnp
import numpy as np
from jax import lax
from jax.experimental import pallas as pl
from jax.experimental.pallas import tpu as pltpu

F32 = jnp.float32
BF16 = jnp.bfloat16

V7X_LANES = 128
V7X_SUBLANES = 8
V7X_MXU_DIM = 256
V7X_VMEM_BYTES = 64 * 1024 * 1024
V7X_VMEM_CEILING = 58 * 1024 * 1024

GRID_W = 64
EPS = 1e-6
ROPE_BASE = 10000.0
FFN_RES = 0.5
MLA_HEADS = 16
MLA_NOPE = 128
MLA_ROPE = 64
MLA_V = 128
LRU_BLOCKS = 16
LRU_C = 8.0
CONV_W = 4
CONV_PAD_L = 2
RET_HEADS = 4
RET_DK = 256
RET_DV = 512
GLA_HEADS = 4
GLA_DK = 256
GLA_DV = 512
GLA_RANK = 16
GLA_TAU = 16.0
CHUNK = 64
LINATTN_HEADS_PER_STEP = 4


def _cparams(sem, vmem_est):
    limit = int(min(V7X_VMEM_CEILING, max(vmem_est * 5 // 4 + (4 << 20), 16 << 20)))
    return pltpu.CompilerParams(dimension_semantics=sem, vmem_limit_bytes=limit)


def _pick(n, cap, mult):
    best = None
    for t in range(mult, min(n, cap) + 1, mult):
        if n % t == 0:
            best = t
    assert best is not None, (n, cap, mult)
    return best


class _Rows:
    def __init__(self, batch, seq, ctx):
        self.B, self.T, self.TC = batch, seq, ctx
        self.n_lat = batch * seq
        self.n_ctx = batch * ctx
        self.R = self.n_lat + self.n_ctx

    def tile(self, cap):
        g = math.gcd(self.T, self.n_ctx)
        return _pick(g, cap, V7X_SUBLANES)

    def set_of_tile(self, i, tm):
        n_lat_tiles = self.n_lat // tm
        per_seq = self.T // tm
        return jnp.where(i < n_lat_tiles, 1 + i // per_seq, 0)


def _ada_body(c_ref, w_ref, b_ref, o_ref):
    s = c_ref[...]
    s = (s * jax.nn.sigmoid(s)).astype(BF16)
    w = w_ref[0].astype(BF16)
    o_ref[0] = jnp.dot(s, w, preferred_element_type=F32) + b_ref[0]


def _ada(c_rows, ada_w, ada_b):
    depth, d, n = ada_w.shape
    rows = c_rows.shape[0]
    tn = _pick(n, 512, V7X_LANES)
    est = 2 * d * tn * 4 + d * tn * 2 + 4 * rows * d * 4
    return pl.pallas_call(
        _ada_body,
        grid=(depth, n // tn),
        in_specs=[
            pl.BlockSpec((rows, d), lambda l, j: (0, 0)),
            pl.BlockSpec((1, d, tn), lambda l, j: (l, 0, j)),
            pl.BlockSpec((1, 1, tn), lambda l, j: (l, 0, j)),
        ],
        out_specs=pl.BlockSpec((1, rows, tn), lambda l, j: (l, 0, j)),
        out_shape=jax.ShapeDtypeStruct((depth, rows, n), F32),
        compiler_params=_cparams(("arbitrary", "arbitrary"), est),
    )(c_rows, ada_w, ada_b.reshape(depth, 1, n))


def _norm_body(h_ref, g_ref, m_ref, o_ref, *, k):
    x = h_ref[...].astype(F32)
    ms = jnp.mean(x * x, axis=-1, keepdims=True)
    y = (x * lax.rsqrt(ms + EPS)) * g_ref[...]
    if k is not None:
        y = y * (1.0 + m_ref[0, 3 * k + 1:3 * k + 2, :]) + m_ref[0, 3 * k:3 * k + 1, :]
    o_ref[...] = y.astype(o_ref.dtype)


def _norm_mod(h, gain, mods, k, rows, out_dtype, n_rows=None):
    n_rows = h.shape[0] if n_rows is None else n_rows
    d = h.shape[1]
    tm = rows.tile(256)
    est = 2 * tm * d * (4 + 4) + 2 * tm * d * 4
    return pl.pallas_call(
        functools.partial(_norm_body, k=k),
        grid=(n_rows // tm,),
        in_specs=[
            pl.BlockSpec((tm, d), lambda i: (i, 0)),
            pl.BlockSpec((1, d), lambda i: (0, 0)),
            pl.BlockSpec((1, 9, d), lambda i: (rows.set_of_tile(i, tm), 0, 0)),
        ],
        out_specs=pl.BlockSpec((tm, d), lambda i: (i, 0)),
        out_shape=jax.ShapeDtypeStruct((n_rows, d), out_dtype),
        compiler_params=_cparams(("arbitrary",), est),
    )(h, gain.reshape(1, d), mods)


def _w_spec(w, lead, kk, tn, row_block=0):
    assert w.ndim == len(lead) + 2
    return pl.BlockSpec((None,) * len(lead) + (kk, tn), lambda j, i: tuple(lead) + (row_block, j))


NORM_ROWS_PER_PASS = 64


def _norm_into(u_scr, h_ref, g_ref, m_ref, gs_scr, k):
    rows, d = u_scr.shape
    ln, sub = V7X_LANES, V7X_SUBLANES
    step = min(NORM_ROWS_PER_PASS, rows)
    gs_scr[0] = jnp.broadcast_to(g_ref[...] * (1.0 + m_ref[0, 3 * k + 1:3 * k + 2, :]), (sub, d))
    gs_scr[1] = jnp.broadcast_to(m_ref[0, 3 * k:3 * k + 1, :], (sub, d))

    def rows_pass(it, carry):
        r0 = pl.multiple_of(it * step, step)
        acc = jnp.zeros((step, ln), F32)
        for c in range(0, d, ln):
            x = h_ref[pl.ds(r0, step), c:c + ln]
            acc = acc + x * x
        ms = jnp.sum(acc, axis=-1, keepdims=True) * (1.0 / d)
        rinv = jnp.broadcast_to(lax.rsqrt(ms + EPS), (step, ln))
        for c in range(0, d, ln):
            x = (h_ref[pl.ds(r0, step), c:c + ln] * rinv).reshape(step // sub, sub, ln)
            y = x * gs_scr[0, :, c:c + ln] + gs_scr[1, :, c:c + ln]
            u_scr[pl.ds(r0, step), c:c + ln] = y.reshape(step, ln).astype(u_scr.dtype)
        return carry

    lax.fori_loop(0, rows // step, rows_pass, 0)


def _mm_up_body(h_ref, g_ref, m_ref, wg_ref, wu_ref, o_ref, u_scr, gs_scr, *, k):
    @pl.when(pl.program_id(1) == 0)
    def _():
        _norm_into(u_scr, h_ref, g_ref, m_ref, gs_scr, k)

    x = u_scr[...]
    g = jnp.dot(x, wg_ref[...], preferred_element_type=F32)
    u = jnp.dot(x, wu_ref[...], preferred_element_type=F32)
    o_ref[...] = (g * jax.nn.sigmoid(g) * u).astype(o_ref.dtype)


def _mm_proj_body(h_ref, g_ref, m_ref, w_ref, o_ref, u_scr, gs_scr, *, k):
    @pl.when(pl.program_id(1) == 0)
    def _():
        _norm_into(u_scr, h_ref, g_ref, m_ref, gs_scr, k)

    o_ref[...] = jnp.dot(u_scr[...], w_ref[...], preferred_element_type=F32).astype(o_ref.dtype)


def _norm_matmul(h, gain, mods, k, ws, lead, rows, tn_cap, swiglu):
    r, d = h.shape
    n = ws[0].shape[-1]
    tm = rows.tile(512)
    tn = _pick(n, tn_cap, V7X_LANES)
    w_specs = [pl.BlockSpec((None,) * len(lead) + (d, tn), lambda i, j: tuple(lead) + (0, j)) for _ in ws]
    est = (2 * tm * d * 4 + tm * d * 2 + 2 * len(ws) * d * tn * 2 + 2 * tm * tn * 2
           + (1 + len(ws)) * tm * tn * 4 + 6 * NORM_ROWS_PER_PASS * d * 4)
    return pl.pallas_call(
        functools.partial(_mm_up_body if swiglu else _mm_proj_body, k=k),
        grid=(r // tm, n // tn),
        in_specs=[
            pl.BlockSpec((tm, d), lambda i, j: (i, 0)),
            pl.BlockSpec((1, d), lambda i, j: (0, 0)),
            pl.BlockSpec((1, 9, d), lambda i, j: (rows.set_of_tile(i, tm), 0, 0)),
        ] + w_specs,
        out_specs=pl.BlockSpec((tm, tn), lambda i, j: (i, j)),
        out_shape=jax.ShapeDtypeStruct((r, n), BF16),
        scratch_shapes=[pltpu.VMEM((tm, d), BF16), pltpu.VMEM((2, V7X_SUBLANES, d), F32)],
        compiler_params=_cparams(("arbitrary", "arbitrary"), est),
    )(h, gain.reshape(1, d), mods, *ws)


def _mm_res_body(*refs, n_pairs, coef, k):
    h_ref, m_ref, o_ref = refs[2 * n_pairs:]
    acc = None
    for p in range(n_pairs):
        part = jnp.dot(refs[2 * p][...], refs[2 * p + 1][...], preferred_element_type=F32)
        acc = part if acc is None else acc + part
    gate = m_ref[0, 3 * k + 2:3 * k + 3, :]
    o_ref[...] = h_ref[...] + (coef * gate) * acc


def _res_matmul(xs, w, lead, h, mods, k, coef, rows):
    r, n = h.shape
    tm = rows.tile(512)
    tn = _pick(n, 512, V7X_LANES)
    in_specs, args, est = [], [], 0
    for p, x in enumerate(xs):
        kk = x.shape[1]
        assert kk * len(xs) == w.shape[-2]
        in_specs += [pl.BlockSpec((tm, kk), lambda j, i: (i, 0)), _w_spec(w, lead, kk, tn, p)]
        args += [x, w]
        est += 2 * tm * kk * 2 + 2 * kk * tn * 2
    in_specs += [pl.BlockSpec((tm, tn), lambda j, i: (i, j)),
                 pl.BlockSpec((1, 9, tn), lambda j, i: (rows.set_of_tile(i, tm), 0, j))]
    est += 6 * tm * tn * 4
    return pl.pallas_call(
        functools.partial(_mm_res_body, n_pairs=len(xs), coef=coef, k=k),
        grid=(n // tn, r // tm),
        in_specs=in_specs,
        out_specs=pl.BlockSpec((tm, tn), lambda j, i: (i, j)),
        out_shape=jax.ShapeDtypeStruct((r, n), F32),
        compiler_params=_cparams(("arbitrary", "arbitrary"), est),
    )(*args, h, mods)


def _rope_pairs(t, c, s1, s2):
    half = MLA_ROPE // 2
    return t * c + pltpu.roll(t, V7X_LANES - half, 1) * s1 + pltpu.roll(t, half, 1) * s2


def _lowrank_body(x_ref, g_ref, w_ref, c_ref, s1_ref, s2_ref, o_ref, *, heads, rope):
    x = x_ref[...].astype(F32)
    ms = jnp.mean(x * x, axis=-1, keepdims=True)
    xn = ((x * lax.rsqrt(ms + EPS)) * g_ref[...]).astype(BF16)
    hw = 2 * V7X_LANES
    for h in range(heads):
        acc = jnp.dot(xn, w_ref[:, h * hw:(h + 1) * hw], preferred_element_type=F32)
        if rope:
            o_ref[:, h * hw:h * hw + V7X_LANES] = acc[:, :V7X_LANES].astype(o_ref.dtype)
            t = _rope_pairs(acc[:, V7X_LANES:], c_ref[...], s1_ref[...], s2_ref[...])
            o_ref[:, h * hw + V7X_LANES:(h + 1) * hw] = t.astype(o_ref.dtype)
        else:
            o_ref[:, h * hw:(h + 1) * hw] = acc.astype(o_ref.dtype)


def _lowrank_up(p, col_block, k, gain, w, tables, rows, heads, rope):
    r = p.shape[0]
    n = w.shape[1]
    tm = rows.tile(512)
    est = 2 * tm * k * 2 + 2 * k * n * 2 + 2 * tm * n * 2 + 6 * tm * V7X_LANES * 4 + tm * k * 8 + 4 * tm * 256 * 4
    tab_spec = pl.BlockSpec((tm, V7X_LANES), lambda i: (i, 0))
    return pl.pallas_call(
        functools.partial(_lowrank_body, heads=heads, rope=rope),
        grid=(r // tm,),
        in_specs=[
            pl.BlockSpec((tm, k), lambda i: (i, col_block)),
            pl.BlockSpec((1, k), lambda i: (0, 0)),
            pl.BlockSpec((k, n), lambda i: (0, 0)),
            tab_spec, tab_spec, tab_spec,
        ],
        out_specs=pl.BlockSpec((tm, n), lambda i: (i, 0)),
        out_shape=jax.ShapeDtypeStruct((r, n), BF16),
        compiler_params=_cparams(("arbitrary",), est),
    )(p, gain.reshape(1, k), w, *tables)


def _kr_rope_body(x_ref, c_ref, s1_ref, s2_ref, o_ref):
    t = _rope_pairs(x_ref[...].astype(F32), c_ref[...], s1_ref[...], s2_ref[...])
    o_ref[...] = t.astype(o_ref.dtype)


def _kr_rope(p, col_block, tables, rows):
    r = p.shape[0]
    tm = rows.tile(512)
    spec = pl.BlockSpec((tm, V7X_LANES), lambda i: (i, 0))
    return pl.pallas_call(
        _kr_rope_body,
        grid=(r // tm,),
        in_specs=[pl.BlockSpec((tm, V7X_LANES), lambda i: (i, col_block)), spec, spec, spec],
        out_specs=spec,
        out_shape=jax.ShapeDtypeStruct((r, V7X_LANES), BF16),
        compiler_params=_cparams(("arbitrary",), 16 * tm * V7X_LANES * 4),
    )(p, *tables)


ATTN_VT_ROWS = V7X_LANES + 16


def _attn_body(*refs, n_ctx, n_lat, tk):
    if n_lat:
        (q_ref, kc_ref, vc_ref, krc_ref, kl_ref, vl_ref, krl_ref,
         o_ref, kcat, vt_c, vt_l, acc_scr, s_a, s_b) = refs
    else:
        q_ref, kc_ref, vc_ref, krc_ref, o_ref, kcat, vt_c, acc_scr, s_a = refs
    ln = V7X_LANES
    nt = (((1,), (1,)), ((), ()))

    @pl.when(pl.program_id(2) == 0)
    def _():
        kcat[0:n_ctx, 0:ln] = kc_ref[...]
        kcat[0:n_ctx, ln:2 * ln] = krc_ref[...]
        eye = (lax.broadcasted_iota(jnp.int32, (ln, ln), 0)
               == lax.broadcasted_iota(jnp.int32, (ln, ln), 1)).astype(BF16)

        def ones_rows(n):
            first = lax.broadcasted_iota(jnp.int32, (ATTN_VT_ROWS - ln, n), 0) == 0
            return jnp.where(first, 1.0, 0.0).astype(BF16)

        vt_c[0:ln, :] = lax.dot_general(eye, vc_ref[...], nt, preferred_element_type=F32).astype(BF16)
        vt_c[ln:, :] = ones_rows(n_ctx)
        if n_lat:
            kcat[n_ctx:n_ctx + n_lat, 0:ln] = kl_ref[...]
            kcat[n_ctx:n_ctx + n_lat, ln:2 * ln] = krl_ref[...]
            for c in range(n_lat // tk):
                vt_l[c, 0:ln, :] = lax.dot_general(eye, vl_ref[c * tk:(c + 1) * tk, :], nt,
                                                   preferred_element_type=F32).astype(BF16)
                vt_l[c, ln:, :] = ones_rows(tk)

    q = q_ref[...]
    tq = q.shape[0]
    acc_scr[...] = jnp.zeros(acc_scr.shape, F32)

    def scores(k, dst):
        dst[0:k.shape[0], :] = lax.dot_general(k, q, nt, preferred_element_type=F32)

    def softmax_pv(src, n, v_t, m_prev):
        s = src[0:n, :]
        m_next = jnp.maximum(m_prev, jnp.max(s, axis=0, keepdims=True))
        p = jnp.exp2(s - m_next).astype(BF16)
        alpha = jnp.exp2(m_prev - m_next)
        acc_scr[...] = alpha * acc_scr[...] + jnp.dot(v_t, p, preferred_element_type=F32)
        return m_next

    def lat_keys(c):
        return kcat[pl.ds(pl.multiple_of(n_ctx + c * tk, ln), tk), :]

    m = jnp.full((1, tq), -jnp.inf, F32)
    scores(kcat[0:n_ctx, :], s_a)
    if not n_lat:
        m = softmax_pv(s_a, n_ctx, vt_c[...], m)
    else:
        n_ch = n_lat // tk
        scores(lat_keys(0), s_b)
        m = softmax_pv(s_a, n_ctx, vt_c[...], m)

        def pair(i, m):
            c = 2 * i
            scores(lat_keys(c + 1), s_a)
            m = softmax_pv(s_b, tk, vt_l[c], m)
            scores(lat_keys(c + 2), s_b)
            return softmax_pv(s_a, tk, vt_l[c + 1], m)

        m = lax.fori_loop(0, n_ch // 2 - 1, pair, m)
        scores(lat_keys(n_ch - 1), s_a)
        m = softmax_pv(s_b, tk, vt_l[n_ch - 2], m)
        m = softmax_pv(s_a, tk, vt_l[n_ch - 1], m)
    out = acc_scr[0:ln, :] / acc_scr[ln:ln + 1, :]
    o_ref[...] = out.T.astype(o_ref.dtype)


def _attention(q, kv, kr, rows, heads, latent):
    b, t, tc = rows.B, rows.T, rows.TC
    ln = V7X_LANES
    ctx_blk0 = rows.n_lat // tc
    if latent:
        tq = _pick(t, 1024, ln)
        nq = t // tq
        tk = _pick(t // 2, 1024, ln)
        assert tk >= tc
        n_lat = t
    else:
        tq, nq, tk, n_lat = tc, 1, tc, 0
    q_row = (lambda bb, qi: bb * nq + qi) if latent else (lambda bb, qi: ctx_blk0 * (tc // tq) + bb)
    in_specs = [
        pl.BlockSpec((tq, 2 * ln), lambda bb, h, qi: (q_row(bb, qi), h)),
        pl.BlockSpec((tc, ln), lambda bb, h, qi: (ctx_blk0 + bb, 2 * h)),
        pl.BlockSpec((tc, ln), lambda bb, h, qi: (ctx_blk0 + bb, 2 * h + 1)),
        pl.BlockSpec((tc, ln), lambda bb, h, qi: (ctx_blk0 + bb, 0)),
    ]
    args = [q, kv, kv, kr]
    scratch = [pltpu.VMEM((tc + n_lat, 2 * ln), BF16), pltpu.VMEM((ATTN_VT_ROWS, tc), BF16)]
    if latent:
        in_specs += [
            pl.BlockSpec((t, ln), lambda bb, h, qi: (bb, 2 * h)),
            pl.BlockSpec((t, ln), lambda bb, h, qi: (bb, 2 * h + 1)),
            pl.BlockSpec((t, ln), lambda bb, h, qi: (bb, 0)),
        ]
        args += [kv, kv, kr]
        scratch.append(pltpu.VMEM((n_lat // tk, ATTN_VT_ROWS, tk), BF16))
    scratch.append(pltpu.VMEM((ATTN_VT_ROWS, tq), F32))
    scratch += [pltpu.VMEM((tk, tq), F32)] * (2 if latent else 1)
    nk = tc + n_lat
    est = (2 * tq * 2 * ln * 2 + 6 * tc * ln * 2 + 6 * n_lat * ln * 2 + nk * 3 * ln * 2
           + 2 * tq * ln * 2 + 2 * tq * ln * 4 + 5 * tq * tk * 4)
    return pl.pallas_call(
        functools.partial(_attn_body, n_ctx=tc, n_lat=n_lat, tk=tk),
        grid=(b, heads, nq),
        in_specs=in_specs,
        out_specs=pl.BlockSpec((tq, ln), lambda bb, h, qi: (q_row(bb, qi), h)),
        out_shape=jax.ShapeDtypeStruct((rows.R, heads * ln), BF16),
        scratch_shapes=scratch,
        compiler_params=_cparams(("arbitrary", "arbitrary", "arbitrary"), est),
    )(*args)


def _scan_rows(a, bv, reverse):
    n = a.shape[0]
    row = lax.broadcasted_iota(jnp.int32, a.shape, 0)
    s = 1
    while s < n:
        if reverse:
            keep = row < (n - s)
            a_sh = jnp.where(keep, pltpu.roll(a, n - s, 0), 1.0)
            b_sh = jnp.where(keep, pltpu.roll(bv, n - s, 0), 0.0)
        else:
            keep = row >= s
            a_sh = jnp.where(keep, pltpu.roll(a, s, 0), 1.0)
            b_sh = jnp.where(keep, pltpu.roll(bv, s, 0), 0.0)
        bv = a * b_sh + bv
        a = a * a_sh
        s *= 2
    return a, bv


def _cumsum_rows(x, reverse):
    n = x.shape[0]
    row = lax.broadcasted_iota(jnp.int32, x.shape, 0)
    s = 1
    while s < n:
        if reverse:
            x = x + jnp.where(row < (n - s), pltpu.roll(x, n - s, 0), 0.0)
        else:
            x = x + jnp.where(row >= s, pltpu.roll(x, s, 0), 0.0)
        s *= 2
    return x


def _gelu_tanh(x):
    return 0.5 * x * (1.0 + jnp.tanh(math.sqrt(2.0 / math.pi) * (x + 0.044715 * (x * x * x))))


def _lru_body(xl_ref, xc_ref, gl_ref, gc_ref, cw_ref, cb_ref, wa_ref, ba_ref, wi_ref, bi_ref, lam_ref,
              yl_ref, yc_ref, cv_l, cv_c, hf_l, hf_c, *, seq, ctx, chunk, conv_chunk):
    pad = V7X_SUBLANES
    zeros8 = jnp.zeros((pad, V7X_LANES), F32)

    def conv_into(src_ref, dst, n, step):
        for c0 in range(0, n, step):
            lo = zeros8 if c0 == 0 else src_ref[c0 - pad:c0, :].astype(F32)
            hi = zeros8 if c0 + step >= n else src_ref[c0 + step:c0 + step + pad, :].astype(F32)
            ext = jnp.concatenate([lo, src_ref[c0:c0 + step, :].astype(F32), hi], axis=0)
            out = cb_ref[...]
            for kk in range(CONV_W):
                off = pad + kk - CONV_PAD_L
                out = out + ext[off:off + step, :] * cw_ref[kk:kk + 1, :]
            dst[c0:c0 + step, :] = out

    conv_into(xl_ref, cv_l, seq, conv_chunk)
    conv_into(xc_ref, cv_c, ctx, min(ctx, conv_chunk))

    for d, reverse in enumerate((False, True)):
        lam = lam_ref[d:d + 1, :]
        neg_sp = -LRU_C * (jnp.maximum(-lam, 0.0) + jnp.log(1.0 + jnp.exp(-jnp.abs(lam))))
        wa = wa_ref[d, 0]
        wi = wi_ref[d, 0]
        ba = ba_ref[d:d + 1, :]
        bi = bi_ref[d:d + 1, :]

        def block(x, carry):
            xg = x.astype(BF16)
            r = jax.nn.sigmoid(jnp.dot(xg, wa, preferred_element_type=F32) + ba)
            i = jax.nn.sigmoid(jnp.dot(xg, wi, preferred_element_type=F32) + bi)
            a = jnp.exp(neg_sp * r)
            bv = jnp.sqrt(1.0 - a * a) * (i * x)
            a_cum, h = _scan_rows(a, bv, reverse)
            h = a_cum * carry + h
            n = x.shape[0]
            new_carry = h[0:1, :] if reverse else h[n - 1:n, :]
            return h, new_carry

        def emit(dst_ref, fwd_scr, g_ref, start, n, h):
            if not reverse:
                fwd_scr[pl.ds(start, n), :] = h
            else:
                tot = fwd_scr[pl.ds(start, n), :] + h
                g = g_ref[pl.ds(start, n), :].astype(F32)
                dst_ref[pl.ds(start, n), :] = (tot * _gelu_tanh(g)).astype(dst_ref.dtype)

        carry = jnp.zeros((1, V7X_LANES), F32)
        cchunk = min(ctx, chunk)
        n_cc = ctx // cchunk
        order = range(n_cc - 1, -1, -1) if reverse else range(n_cc)
        lat_init = None
        for c in order:
            h, carry = block(cv_c[c * cchunk:(c + 1) * cchunk, :], carry)
            if lat_init is None:
                lat_init = h[cchunk - 1:cchunk, :] if reverse else h[0:1, :]
            emit(yc_ref, hf_c, gc_ref, c * cchunk, cchunk, h)
        carry = lat_init

        n_lc = seq // chunk

        def body(it, carry):
            c = (n_lc - 1 - it) if reverse else it
            start = pl.multiple_of(c * chunk, chunk)
            h, carry = block(cv_l[pl.ds(start, chunk), :], carry)
            emit(yl_ref, hf_l, gl_ref, start, chunk, h)
            return carry

        lax.fori_loop(0, n_lc, body, carry)


def _rglru(p, xb_col, gb_col, width, conv_w, conv_b, w_a, b_a, w_i, b_i, lam, rows):
    b, t, tc = rows.B, rows.T, rows.TC
    ln = V7X_LANES
    groups = width // ln
    ctx_blk0 = rows.n_lat // tc
    chunk = _pick(t, 256, V7X_SUBLANES)
    conv_chunk = _pick(t, 1024, V7X_SUBLANES)
    vec = lambda rws: pl.BlockSpec((rws, ln), lambda bb, g: (0, g))
    est = 2 * (t + tc) * ln * (2 + 2 + 2) + 2 * (t + tc) * ln * 4 + 64 * chunk * ln * 4
    return pl.pallas_call(
        functools.partial(_lru_body, seq=t, ctx=tc, chunk=chunk, conv_chunk=conv_chunk),
        grid=(b, groups),
        in_specs=[
            pl.BlockSpec((t, ln), lambda bb, g: (bb, xb_col + g)),
            pl.BlockSpec((tc, ln), lambda bb, g: (ctx_blk0 + bb, xb_col + g)),
            pl.BlockSpec((t, ln), lambda bb, g: (bb, gb_col + g)),
            pl.BlockSpec((tc, ln), lambda bb, g: (ctx_blk0 + bb, gb_col + g)),
            vec(CONV_W), vec(1),
            pl.BlockSpec((2, 1, ln, ln), lambda bb, g: (0, g, 0, 0)), vec(2),
            pl.BlockSpec((2, 1, ln, ln), lambda bb, g: (0, g, 0, 0)), vec(2),
            vec(2),
        ],
        out_specs=[
            pl.BlockSpec((t, ln), lambda bb, g: (bb, g)),
            pl.BlockSpec((tc, ln), lambda bb, g: (bb, g)),
        ],
        out_shape=[jax.ShapeDtypeStruct((rows.R, width), BF16),
                   jax.ShapeDtypeStruct((b * tc, width), BF16)],
        scratch_shapes=[pltpu.VMEM((t, ln), F32), pltpu.VMEM((tc, ln), F32),
                        pltpu.VMEM((t, ln), F32), pltpu.VMEM((tc, ln), F32)],
        compiler_params=_cparams(("arbitrary", "arbitrary"), est),
    )(p, p, p, p, conv_w, conv_b.reshape(1, width), w_a, b_a, w_i, b_i, lam)


def _log_sigmoid(x):
    return jnp.minimum(x, 0.0) - jnp.log(1.0 + jnp.exp(-jnp.abs(x)))


def _linattn_body(*refs, kind, reverse, tb, dk, dv):
    it = iter(refs)
    q_ref, k_ref, v_ref = next(it), next(it), next(it)
    if kind == "ret":
        cos_ref, sin_ref, dec_ref = next(it), next(it), next(it)
    else:
        ga_ref, wg_ref, bg_ref = next(it), next(it), next(it)
    if reverse:
        of_ref, g_ref, gain_ref = next(it), next(it), next(it)
    o_ref, st = next(it), next(it)
    hp = st.shape[0]

    @pl.when(pl.program_id(2) == 0)
    def _():
        st[...] = jnp.zeros(st.shape, F32)

    n_ch = tb // CHUNK
    ri = lax.broadcasted_iota(jnp.int32, (CHUNK, CHUNK), 0)
    ci = lax.broadcasted_iota(jnp.int32, (CHUNK, CHUNK), 1)
    mask = (ci >= ri) if reverse else (ci <= ri)
    nt = (((1,), (1,)), ((), ()))
    tn = (((0,), (0,)), ((), ()))
    pos = lax.broadcasted_iota(jnp.int32, (tb, 1), 0) % CHUNK
    n_terms = ((CHUNK - pos) if reverse else (pos + 1)).astype(F32)
    for hh in range(hp):
        q = q_ref[:, hh * dk:(hh + 1) * dk].astype(F32)
        k = k_ref[:, hh * dk:(hh + 1) * dk].astype(F32)
        if kind == "ret":
            c, s = cos_ref[...], sin_ref[...]
            hd = dk // 2
            q = jnp.concatenate([q[:, :hd] * c - q[:, hd:] * s, q[:, :hd] * s + q[:, hd:] * c], axis=1)
            k = jnp.concatenate([k[:, :hd] * c - k[:, hd:] * s, k[:, :hd] * s + k[:, hd:] * c], axis=1)
            bcum_all = n_terms * dec_ref[hh]
        else:
            z = jnp.dot(ga_ref[...], wg_ref[0, :, hh * dk:(hh + 1) * dk], preferred_element_type=F32)
            la = _log_sigmoid(z + bg_ref[0, :, hh * dk:(hh + 1) * dk]) * (1.0 / GLA_TAU)
        outs = [None] * n_ch
        for c in (range(n_ch - 1, -1, -1) if reverse else range(n_ch)):
            sl = slice(c * CHUNK, (c + 1) * CHUNK)
            bcum = bcum_all[sl] if kind == "ret" else _cumsum_rows(la[sl], reverse)
            bend = bcum[0:1, :] if reverse else bcum[CHUNK - 1:CHUNK, :]
            qe = (q[sl] * jnp.exp(bcum)).astype(BF16)
            ke = (k[sl] * jnp.exp(-bcum)).astype(BF16)
            ks = (k[sl] * jnp.exp(bend - bcum)).astype(BF16)
            vc = v_ref[sl, hh * dv:(hh + 1) * dv]
            att = lax.dot_general(qe, ke, nt, preferred_element_type=F32)
            att = jnp.where(mask, att, 0.0).astype(BF16)
            o = jnp.dot(att, vc, preferred_element_type=F32)
            o = o + lax.dot_general(qe, st[hh].astype(BF16), nt, preferred_element_type=F32)
            st[hh] = st[hh] * jnp.exp(bend) + lax.dot_general(vc, ks, tn, preferred_element_type=F32)
            outs[c] = o
        o = jnp.concatenate(outs, axis=0)
        cs = slice(hh * dv, (hh + 1) * dv)
        if not reverse:
            o_ref[:, cs] = o
        else:
            y = of_ref[:, cs] + o
            ms = jnp.mean(y * y, axis=-1, keepdims=True)
            y = (y * lax.rsqrt(ms + EPS)) * gain_ref[:, cs]
            g = g_ref[:, cs].astype(F32)
            o_ref[:, cs] = (y * (g * jax.nn.sigmoid(g))).astype(o_ref.dtype)


def _linattn(p, cols, heads, kind, reverse, rows, extra, o_fwd=None, gain=None):
    b, t, tc = rows.B, rows.T, rows.TC
    dk, dv = RET_DK, RET_DV
    tb = tc
    assert t % tb == 0 and tb % CHUNK == 0
    nt = 1 + t // tb
    ctx_blk0 = rows.n_lat // tb
    per_seq = t // tb

    def row_blk(bb, tt):
        lat = (per_seq - tt) if reverse else (tt - 1)
        return jnp.where(tt == 0, ctx_blk0 + bb, bb * per_seq + lat)

    qc, kc, vc, gc = cols
    hp = math.gcd(heads, LINATTN_HEADS_PER_STEP)
    wk, wv = hp * dk, hp * dv
    assert all(o % wk == 0 for o in (qc, kc)) and all(o % wv == 0 for o in (vc, gc))
    in_specs = [
        pl.BlockSpec((tb, wk), lambda bb, h, tt: (row_blk(bb, tt), qc // wk + h)),
        pl.BlockSpec((tb, wk), lambda bb, h, tt: (row_blk(bb, tt), kc // wk + h)),
        pl.BlockSpec((tb, wv), lambda bb, h, tt: (row_blk(bb, tt), vc // wv + h)),
    ]
    args = [p, p, p]
    if kind == "ret":
        cos, sin, dec = extra
        tab = pl.BlockSpec((tb, dk // 2), lambda bb, h, tt: (row_blk(bb, tt), 0))
        in_specs += [tab, tab, pl.BlockSpec((hp, 1, dk), lambda bb, h, tt: (h, 0, 0))]
        args += [cos, sin, dec]
    else:
        ga_col, wg, bg = extra
        in_specs += [
            pl.BlockSpec((tb, V7X_LANES), lambda bb, h, tt: (row_blk(bb, tt), ga_col // V7X_LANES)),
            pl.BlockSpec((1, V7X_LANES, wk), lambda bb, h, tt: (0, 0, h)),
            pl.BlockSpec((1, 1, wk), lambda bb, h, tt: (0, 0, h)),
        ]
        args += [p, wg, bg]
    if reverse:
        in_specs += [
            pl.BlockSpec((tb, wv), lambda bb, h, tt: (row_blk(bb, tt), h)),
            pl.BlockSpec((tb, wv), lambda bb, h, tt: (row_blk(bb, tt), gc // wv + h)),
            pl.BlockSpec((1, wv), lambda bb, h, tt: (0, h)),
        ]
        args += [o_fwd, p, gain]
    est = hp * (2 * tb * (2 * dk + dv) * 2 + 6 * tb * dv * 4 + 3 * dv * dk * 4 + 24 * tb * dk * 4 + 4 * tb * dv * 4)
    return pl.pallas_call(
        functools.partial(_linattn_body, kind=kind, reverse=reverse, tb=tb, dk=dk, dv=dv),
        grid=(b, heads // hp, nt),
        in_specs=in_specs,
        out_specs=pl.BlockSpec((tb, wv), lambda bb, h, tt: (row_blk(bb, tt), h)),
        out_shape=jax.ShapeDtypeStruct((rows.R, heads * dv), BF16 if reverse else F32),
        scratch_shapes=[pltpu.VMEM((hp, dv, dk), F32)],
        compiler_params=_cparams(("arbitrary", "arbitrary", "arbitrary"), est),
    )(*args)


def _pad_cols(w, n):
    return jnp.pad(w, ((0, 0), (0, n - w.shape[1])))


def _axial_angles(t, rot_dim):
    n_rows = t // GRID_W
    row = jnp.repeat(jnp.arange(n_rows, dtype=F32), GRID_W)
    col = jnp.tile(jnp.arange(GRID_W, dtype=F32), n_rows)
    n_freq = rot_dim // 4
    inv = ROPE_BASE ** (-jnp.arange(n_freq, dtype=F32) / n_freq)
    return jnp.concatenate([row[:, None] * inv, col[:, None] * inv], axis=-1)


def _row_tables(rows, rot_dim):
    ang = _axial_angles(rows.T, rot_dim)
    half = rot_dim // 2
    cos = jnp.concatenate([jnp.tile(jnp.cos(ang), (rows.B, 1)), jnp.ones((rows.n_ctx, half), F32)], axis=0)
    sin = jnp.concatenate([jnp.tile(jnp.sin(ang), (rows.B, 1)), jnp.zeros((rows.n_ctx, half), F32)], axis=0)
    return cos, sin


def _mla_tables(rows):
    cos, sin = _row_tables(rows, MLA_ROPE)
    half = MLA_ROPE // 2
    z = jnp.zeros((rows.R, V7X_LANES - MLA_ROPE), F32)
    zh = jnp.zeros((rows.R, half), F32)
    c = jnp.concatenate([cos, cos, z], axis=1)
    s1 = jnp.concatenate([-sin, zh, z], axis=1)
    s2 = jnp.concatenate([zh, sin, z], axis=1)
    return c, s1, s2


def _even_mixer(h, gain, mods, rows, w_in, q_norm, w_uq, kv_norm, w_ukv, conv_w, conv_b,
                w_a, b_a, w_i, b_i, lam, w_out, tables):
    q_rank, kv_rank, lru_w = q_norm.shape[0], kv_norm.shape[0], conv_w.shape[1]
    ln = V7X_LANES
    o_cq, o_ckv, o_kr, o_xb, o_gb = np.cumsum([0, q_rank, kv_rank, MLA_ROPE, lru_w]).tolist()
    w_cols = jnp.concatenate([w_in[:, :o_kr], w_in[:, o_xb:], w_in[:, o_kr:o_xb]], axis=1)
    n_real = w_cols.shape[1] + (ln - MLA_ROPE)
    n_pad = -(-n_real // (5 * V7X_MXU_DIM)) * (5 * V7X_MXU_DIM)
    p = _norm_matmul(h, gain, mods, 1, (_pad_cols(w_cols, n_pad).astype(BF16),), (), rows,
                     5 * V7X_MXU_DIM, False)
    c_cq, c_ckv, c_xb, c_gb, c_kr = 0, q_rank, q_rank + kv_rank, q_rank + kv_rank + lru_w, q_rank + kv_rank + 2 * lru_w

    scale = (MLA_NOPE + MLA_ROPE) ** -0.5 * math.log2(math.e)
    wq = (w_uq * scale).reshape(q_rank, MLA_HEADS, MLA_NOPE + MLA_ROPE)
    wq = jnp.pad(wq, ((0, 0), (0, 0), (0, 2 * ln - MLA_NOPE - MLA_ROPE))).reshape(q_rank, MLA_HEADS * 2 * ln)
    assert c_cq % q_rank == 0 and c_ckv % kv_rank == 0
    q = _lowrank_up(p, c_cq // q_rank, q_rank, q_norm, wq.astype(BF16), tables, rows, MLA_HEADS, True)
    kv = _lowrank_up(p, c_ckv // kv_rank, kv_rank, kv_norm, w_ukv.astype(BF16), tables, rows, MLA_HEADS, False)
    kr = _kr_rope(p, c_kr // ln, tables, rows)

    att = _attention(q, kv, kr, rows, MLA_HEADS, latent=True)
    att_c = _attention(q, kv, kr, rows, MLA_HEADS, latent=False)
    att = lax.dynamic_update_slice(att, att_c[rows.n_lat:], (rows.n_lat, 0))

    y_l, y_c = _rglru(p, c_xb // ln, c_gb // ln, lru_w, conv_w, conv_b,
                      w_a.astype(BF16), b_a, w_i.astype(BF16), b_i, lam, rows)
    lru = lax.dynamic_update_slice(y_l, y_c, (rows.n_lat, 0))
    n_att = MLA_HEADS * MLA_V
    assert MLA_HEADS * MLA_V == lru_w
    return _res_matmul([att, lru], w_out.astype(BF16), (), h, mods, 1, 1.0, rows)


def _odd_mixer(h, gain, mods, rows, w_in, ret_log_decay, ret_norm, gla_w_gate2, gla_b_gate, gla_norm, w_out,
               tables):
    ln = V7X_LANES
    sizes = [RET_HEADS * RET_DK, RET_HEADS * RET_DK, RET_HEADS * RET_DV, RET_HEADS * RET_DV,
             GLA_HEADS * GLA_DK, GLA_HEADS * GLA_DK, GLA_HEADS * GLA_DV, GLA_HEADS * GLA_DV]
    offs = np.cumsum([0] + sizes).tolist()
    rq, rk, rv, rg, gq, gk, gv, gr, ga = offs
    col_scale = jnp.ones((w_in.shape[1],), F32)
    col_scale = col_scale.at[rk:rv].set(RET_DK ** -0.5).at[gq:gk].set(GLA_DK ** -0.5)
    n_real = ga + ln
    n_pad = -(-n_real // (5 * V7X_MXU_DIM)) * (5 * V7X_MXU_DIM)
    p = _norm_matmul(h, gain, mods, 1, (_pad_cols(w_in * col_scale, n_pad).astype(BF16),), (), rows,
                     5 * V7X_MXU_DIM, False)

    cos, sin = tables
    ys = []
    for kind, heads, cols, norm in (("ret", RET_HEADS, (rq, rk, rv, rg), ret_norm),
                                    ("gla", GLA_HEADS, (gq, gk, gv, gr), gla_norm)):
        o_f = None
        for d, reverse in enumerate((False, True)):
            if kind == "ret":
                dec = jnp.broadcast_to(ret_log_decay[d][:, None, None], (heads, 1, RET_DK)).astype(F32)
                extra = (cos, sin, dec)
            else:
                wg = jnp.zeros((1, ln, heads * GLA_DK), F32).at[0, d * GLA_RANK:(d + 1) * GLA_RANK].set(gla_w_gate2[d])
                extra = (ga, wg.astype(BF16), gla_b_gate[d].reshape(1, 1, heads * GLA_DK))
            out = _linattn(p, cols, heads, kind, reverse, rows, extra, o_fwd=o_f,
                           gain=norm.reshape(1, heads * RET_DV))
            if reverse:
                ys.append(out)
            else:
                o_f = out
    n_ret = RET_HEADS * RET_DV
    assert RET_HEADS * RET_DV == GLA_HEADS * GLA_DV
    return _res_matmul(ys, w_out.astype(BF16), (), h, mods, 1, 1.0, rows)


def kernel(x, c, ctx, c_ctx, ada_w, ada_b, norm_w, ffn_w_gate, ffn_w_up, ffn_w_down, ev_w_in, mla_q_norm, mla_w_uq, mla_kv_norm, mla_w_ukv, lru_conv_w, lru_conv_b, lru_w_a, lru_b_a, lru_w_i, lru_b_i, lru_lambda, ev_w_out, od_w_in, ret_log_decay, ret_norm, gla_w_gate2, gla_b_gate, gla_norm, od_w_out, final_norm_w):
    batch, seq, d = x.shape
    tc = ctx.shape[1]
    depth = ada_w.shape[0]
    rows = _Rows(batch, seq, tc)
    h = jnp.concatenate([x.reshape(batch * seq, d), ctx.reshape(batch * tc, d)], axis=0)

    n_sets = 1 + batch
    c_rows = jnp.concatenate([c_ctx[None, :], c], axis=0)
    c_rows = jnp.pad(c_rows, ((0, -n_sets % V7X_SUBLANES), (0, 0)))
    mods_all = _ada(c_rows, ada_w, ada_b)[:, :n_sets].reshape(depth, n_sets, 9, d)

    mla_tabs = _mla_tables(rows)
    ret_tabs = _row_tables(rows, RET_DK)
    w_gate_bf, w_up_bf, w_down_bf = (w.astype(BF16) for w in (ffn_w_gate, ffn_w_up, ffn_w_down))

    for l in range(depth):
        mods = mods_all[l]

        def ffn(hh, k, idx):
            a = _norm_matmul(hh, norm_w[l, k], mods, k, (w_gate_bf, w_up_bf), (l, idx), rows, 512, True)
            return _res_matmul([a], w_down_bf, (l, idx), hh, mods, k, FFN_RES, rows)

        h = ffn(h, 0, 0)
        if l % 2 == 0:
            e = l // 2
            h = _even_mixer(h, norm_w[l, 1], mods, rows, ev_w_in[e], mla_q_norm[e], mla_w_uq[e], mla_kv_norm[e],
                            mla_w_ukv[e], lru_conv_w[e], lru_conv_b[e], lru_w_a[e], lru_b_a[e],
                            lru_w_i[e], lru_b_i[e], lru_lambda[e], ev_w_out[e], mla_tabs)
        else:
            o = l // 2
            h = _odd_mixer(h, norm_w[l, 1], mods, rows, od_w_in[o], ret_log_decay[o], ret_norm[o], gla_w_gate2[o],
                           gla_b_gate[o], gla_norm[o], od_w_out[o], ret_tabs)
        h = ffn(h, 2, 1)

    out = _norm_mod(h, final_norm_w, mods_all[0], None, rows, F32, n_rows=rows.n_lat)
    return out.reshape(batch, seq, d)
```

```python
import functools
import math

import jax
import jax.numpy as jnp
import numpy as np
from jax import lax
from jax.experimental import pallas as pl
from jax.experimental.pallas import tpu as pltpu

F32 = jnp.float32
BF16 = jnp.bfloat16

V7X_LANES = 128
V7X_SUBLANES = 8
V7X_MXU_DIM = 256
V7X_VMEM_BYTES = 64 * 1024 * 1024
V7X_VMEM_CEILING = 58 * 1024 * 1024

GRID_W = 64
EPS = 1e-6
ROPE_BASE = 10000.0
FFN_RES = 0.5
MLA_HEADS = 16
MLA_NOPE = 128
MLA_ROPE = 64
MLA_V = 128
LRU_BLOCKS = 16
LRU_C = 8.0
CONV_W = 4
CONV_PAD_L = 2
RET_HEADS = 4
RET_DK = 256
RET_DV = 512
GLA_HEADS = 4
GLA_DK = 256
GLA_DV = 512
GLA_RANK = 16
GLA_TAU = 16.0
CHUNK = 64
LINATTN_HEADS_PER_STEP = 4


def _cparams(sem, vmem_est):
    limit = int(min(V7X_VMEM_CEILING, max(vmem_est * 5 // 4 + (4 << 20), 16 << 20)))
    return pltpu.CompilerParams(dimension_semantics=sem, vmem_limit_bytes=limit)


def _pick(n, cap, mult):
    best = None
    for t in range(mult, min(n, cap) + 1, mult):
        if n % t == 0:
            best = t
    assert best is not None, (n, cap, mult)
    return best


class _Rows:
    def __init__(self, batch, seq, ctx):
        self.B, self.T, self.TC = batch, seq, ctx
        self.n_lat = batch * seq
        self.n_ctx = batch * ctx
        self.R = self.n_lat + self.n_ctx

    def tile(self, cap):
        g = math.gcd(self.T, self.n_ctx)
        return _pick(g, cap, V7X_SUBLANES)

    def set_of_tile(self, i, tm):
        n_lat_tiles = self.n_lat // tm
        per_seq = self.T // tm
        return jnp.where(i < n_lat_tiles, 1 + i // per_seq, 0)


def _ada_body(c_ref, w_ref, b_ref, o_ref):
    s = c_ref[...]
    s = (s * jax.nn.sigmoid(s)).astype(BF16)
    w = w_ref[0].astype(BF16)
    o_ref[0] = jnp.dot(s, w, preferred_element_type=F32) + b_ref[0]


def _ada(c_rows, ada_w, ada_b):
    depth, d, n = ada_w.shape
    rows = c_rows.shape[0]
    tn = _pick(n, 512, V7X_LANES)
    est = 2 * d * tn * 4 + d * tn * 2 + 4 * rows * d * 4
    return pl.pallas_call(
        _ada_body,
        grid=(depth, n // tn),
        in_specs=[
            pl.BlockSpec((rows, d), lambda l, j: (0, 0)),
            pl.BlockSpec((1, d, tn), lambda l, j: (l, 0, j)),
            pl.BlockSpec((1, 1, tn), lambda l, j: (l, 0, j)),
        ],
        out_specs=pl.BlockSpec((1, rows, tn), lambda l, j: (l, 0, j)),
        out_shape=jax.ShapeDtypeStruct((depth, rows, n), F32),
        compiler_params=_cparams(("arbitrary", "arbitrary"), est),
    )(c_rows, ada_w, ada_b.reshape(depth, 1, n))


def _h_specs(h, rows, tm, tn, tile_of):
    if not isinstance(h, tuple):
        return [pl.BlockSpec((tm, tn), tile_of)], [h]
    n_lat_tiles = rows.n_lat // tm

    def lat(*g):
        i, j = tile_of(*g)
        return jnp.minimum(i, n_lat_tiles - 1), j

    def ctx(*g):
        i, j = tile_of(*g)
        return jnp.maximum(i - n_lat_tiles, 0), j

    return [pl.BlockSpec((tm, tn), lat), pl.BlockSpec((tm, tn), ctx)], list(h)


def _read_h(h_refs, row_tile, n_lat_tiles):
    if len(h_refs) == 1:
        return h_refs[0][...]
    return jnp.where(row_tile < n_lat_tiles, h_refs[0][...], h_refs[1][...])


def _norm_body(*refs, k, n_h, n_lat_tiles):
    g_ref, m_ref, o_ref = refs[n_h:]
    x = _read_h(refs[:n_h], pl.program_id(0), n_lat_tiles)
    ms = jnp.mean(x * x, axis=-1, keepdims=True)
    y = (x * lax.rsqrt(ms + EPS)) * g_ref[...]
    if k is not None:
        y = y * (1.0 + m_ref[0, 3 * k + 1:3 * k + 2, :]) + m_ref[0, 3 * k:3 * k + 1, :]
    o_ref[...] = y.astype(o_ref.dtype)


def _norm_mod(h, gain, mods, k, rows, out_dtype, n_rows):
    d = gain.shape[0]
    tm = rows.tile(256)
    h_specs, h_args = _h_specs(h, rows, tm, d, lambda i: (i, 0))
    est = 2 * tm * d * (4 * len(h_args) + 4) + 2 * tm * d * 4
    return pl.pallas_call(
        functools.partial(_norm_body, k=k, n_h=len(h_args), n_lat_tiles=rows.n_lat // tm),
        grid=(n_rows // tm,),
        in_specs=h_specs + [
            pl.BlockSpec((1, d), lambda i: (0, 0)),
            pl.BlockSpec((1, 9, d), lambda i: (rows.set_of_tile(i, tm), 0, 0)),
        ],
        out_specs=pl.BlockSpec((tm, d), lambda i: (i, 0)),
        out_shape=jax.ShapeDtypeStruct((n_rows, d), out_dtype),
        compiler_params=_cparams(("arbitrary",), est),
    )(*h_args, gain.reshape(1, d), mods)


def _w_spec(w, lead, kk, tn, row_block=0):
    assert w.ndim == len(lead) + 2
    return pl.BlockSpec((None,) * len(lead) + (kk, tn), lambda j, i: tuple(lead) + (row_block, j))


NORM_ROWS_PER_PASS = 64


def _norm_into(u_scr, h_ref, g_ref, m_ref, gs_scr, k):
    rows, d = u_scr.shape
    ln, sub = V7X_LANES, V7X_SUBLANES
    step = min(NORM_ROWS_PER_PASS, rows)
    gs_scr[0] = jnp.broadcast_to(g_ref[...] * (1.0 + m_ref[0, 3 * k + 1:3 * k + 2, :]), (sub, d))
    gs_scr[1] = jnp.broadcast_to(m_ref[0, 3 * k:3 * k + 1, :], (sub, d))

    def rows_pass(it, carry):
        r0 = pl.multiple_of(it * step, step)
        acc = jnp.zeros((step, ln), F32)
        for c in range(0, d, ln):
            x = h_ref[pl.ds(r0, step), c:c + ln]
            acc = acc + x * x
        ms = jnp.sum(acc, axis=-1, keepdims=True) * (1.0 / d)
        rinv = jnp.broadcast_to(lax.rsqrt(ms + EPS), (step, ln))
        for c in range(0, d, ln):
            x = (h_ref[pl.ds(r0, step), c:c + ln] * rinv).reshape(step // sub, sub, ln)
            y = x * gs_scr[0, :, c:c + ln] + gs_scr[1, :, c:c + ln]
            u_scr[pl.ds(r0, step), c:c + ln] = y.reshape(step, ln).astype(u_scr.dtype)
        return carry

    lax.fori_loop(0, rows // step, rows_pass, 0)


def _mm_proj_body(h_ref, g_ref, m_ref, w_ref, o_ref, u_scr, gs_scr, *, k):
    @pl.when(pl.program_id(1) == 0)
    def _():
        _norm_into(u_scr, h_ref, g_ref, m_ref, gs_scr, k)

    o_ref[...] = jnp.dot(u_scr[...], w_ref[...], preferred_element_type=F32).astype(o_ref.dtype)


def _norm_matmul(h, gain, mods, k, w, rows, tn_cap):
    r, d = h.shape
    n = w.shape[-1]
    tm = rows.tile(512)
    tn = _pick(n, tn_cap, V7X_LANES)
    est = (2 * tm * d * 4 + tm * d * 2 + 2 * d * tn * 2 + 2 * tm * tn * 2
           + 2 * tm * tn * 4 + 6 * NORM_ROWS_PER_PASS * d * 4)
    return pl.pallas_call(
        functools.partial(_mm_proj_body, k=k),
        grid=(r // tm, n // tn),
        in_specs=[
            pl.BlockSpec((tm, d), lambda i, j: (i, 0)),
            pl.BlockSpec((1, d), lambda i, j: (0, 0)),
            pl.BlockSpec((1, 9, d), lambda i, j: (rows.set_of_tile(i, tm), 0, 0)),
            pl.BlockSpec((d, tn), lambda i, j: (0, j)),
        ],
        out_specs=pl.BlockSpec((tm, tn), lambda i, j: (i, j)),
        out_shape=jax.ShapeDtypeStruct((r, n), BF16),
        scratch_shapes=[pltpu.VMEM((tm, d), BF16), pltpu.VMEM((2, V7X_SUBLANES, d), F32)],
        compiler_params=_cparams(("arbitrary", "arbitrary"), est),
    )(h, gain.reshape(1, d), mods, w)


CAST_ROWS_PER_PASS = 512


def _mm_up_cast_body(x_ref, wg_ref, wu_ref, o_ref, wg_scr, wu_scr):
    @pl.when(pl.program_id(1) == 0)
    def _():
        kk = wg_scr.shape[0]
        step = min(CAST_ROWS_PER_PASS, kk)
        for r0 in range(0, kk, step):
            wg_scr[r0:r0 + step, :] = wg_ref[r0:r0 + step, :].astype(BF16)
            wu_scr[r0:r0 + step, :] = wu_ref[r0:r0 + step, :].astype(BF16)

    x = x_ref[...]
    g = jnp.dot(x, wg_scr[...], preferred_element_type=F32)
    u = jnp.dot(x, wu_scr[...], preferred_element_type=F32)
    o_ref[...] = (g * jax.nn.sigmoid(g) * u).astype(o_ref.dtype)


def _ffn_up(u, w_gate, w_up, lead, rows):
    r, k = u.shape
    n = w_gate.shape[-1]
    tm = rows.tile(512)
    tn = _pick(n, 512, V7X_LANES)
    est = 2 * tm * k * 2 + 4 * k * tn * 4 + 2 * k * tn * 2 + 2 * tm * tn * 2 + 2 * tm * tn * 4
    return pl.pallas_call(
        _mm_up_cast_body,
        grid=(n // tn, r // tm),
        in_specs=[
            pl.BlockSpec((tm, k), lambda j, i: (i, 0)),
            _w_spec(w_gate, lead, k, tn),
            _w_spec(w_up, lead, k, tn),
        ],
        out_specs=pl.BlockSpec((tm, tn), lambda j, i: (i, j)),
        out_shape=jax.ShapeDtypeStruct((r, n), BF16),
        scratch_shapes=[pltpu.VMEM((k, tn), BF16), pltpu.VMEM((k, tn), BF16)],
        compiler_params=_cparams(("arbitrary", "arbitrary"), est),
    )(u, w_gate, w_up)


def _mm_res_body(*refs, n_pairs, n_h, n_lat_tiles, coef, k):
    h_refs = refs[2 * n_pairs:2 * n_pairs + n_h]
    m_ref, o_ref = refs[2 * n_pairs + n_h:]
    acc = None
    for p in range(n_pairs):
        part = jnp.dot(refs[2 * p][...], refs[2 * p + 1][...], preferred_element_type=F32)
        acc = part if acc is None else acc + part
    gate = m_ref[0, 3 * k + 2:3 * k + 3, :]
    o_ref[...] = _read_h(h_refs, pl.program_id(1), n_lat_tiles) + (coef * gate) * acc


def _res_matmul(xs, w, lead, h, mods, k, coef, rows, n_rows):
    n = w.shape[-1]
    tm = rows.tile(512)
    tn = _pick(n, 512, V7X_LANES)
    in_specs, args, est = [], [], 0
    for p, x in enumerate(xs):
        kk = x.shape[1]
        assert kk * len(xs) == w.shape[-2]
        in_specs += [pl.BlockSpec((tm, kk), lambda j, i: (i, 0)), _w_spec(w, lead, kk, tn, p)]
        args += [x, w]
        est += 2 * tm * kk * 2 + 2 * kk * tn * 2
    h_specs, h_args = _h_specs(h, rows, tm, tn, lambda j, i: (i, j))
    in_specs += h_specs + [pl.BlockSpec((1, 9, tn), lambda j, i: (rows.set_of_tile(i, tm), 0, j))]
    est += (4 + 2 * len(h_args)) * tm * tn * 4
    return pl.pallas_call(
        functools.partial(_mm_res_body, n_pairs=len(xs), n_h=len(h_args),
                          n_lat_tiles=rows.n_lat // tm, coef=coef, k=k),
        grid=(n // tn, n_rows // tm),
        in_specs=in_specs,
        out_specs=pl.BlockSpec((tm, tn), lambda j, i: (i, j)),
        out_shape=jax.ShapeDtypeStruct((rows.R, n), F32),
        compiler_params=_cparams(("arbitrary", "arbitrary"), est),
    )(*args, *h_args, mods)


def _rope_pairs(t, c, s1, s2):
    half = MLA_ROPE // 2
    return t * c + pltpu.roll(t, V7X_LANES - half, 1) * s1 + pltpu.roll(t, half, 1) * s2


def _lowrank_body(x_ref, g_ref, w_ref, c_ref, s1_ref, s2_ref, o_ref, *, heads, rope):
    x = x_ref[...].astype(F32)
    ms = jnp.mean(x * x, axis=-1, keepdims=True)
    xn = ((x * lax.rsqrt(ms + EPS)) * g_ref[...]).astype(BF16)
    hw = 2 * V7X_LANES
    for h in range(heads):
        acc = jnp.dot(xn, w_ref[:, h * hw:(h + 1) * hw], preferred_element_type=F32)
        if rope:
            o_ref[:, h * hw:h * hw + V7X_LANES] = acc[:, :V7X_LANES].astype(o_ref.dtype)
            t = _rope_pairs(acc[:, V7X_LANES:], c_ref[...], s1_ref[...], s2_ref[...])
            o_ref[:, h * hw + V7X_LANES:(h + 1) * hw] = t.astype(o_ref.dtype)
        else:
            o_ref[:, h * hw:(h + 1) * hw] = acc.astype(o_ref.dtype)


def _lowrank_up(p, col_block, k, gain, w, tables, rows, heads, rope):
    r = p.shape[0]
    n = w.shape[1]
    tm = rows.tile(512)
    est = 2 * tm * k * 2 + 2 * k * n * 2 + 2 * tm * n * 2 + 6 * tm * V7X_LANES * 4 + tm * k * 8 + 4 * tm * 256 * 4
    tab_spec = pl.BlockSpec((tm, V7X_LANES), lambda i: (i, 0))
    return pl.pallas_call(
        functools.partial(_lowrank_body, heads=heads, rope=rope),
        grid=(r // tm,),
        in_specs=[
            pl.BlockSpec((tm, k), lambda i: (i, col_block)),
            pl.BlockSpec((1, k), lambda i: (0, 0)),
            pl.BlockSpec((k, n), lambda i: (0, 0)),
            tab_spec, tab_spec, tab_spec,
        ],
        out_specs=pl.BlockSpec((tm, n), lambda i: (i, 0)),
        out_shape=jax.ShapeDtypeStruct((r, n), BF16),
        compiler_params=_cparams(("arbitrary",), est),
    )(p, gain.reshape(1, k), w, *tables)


def _kr_rope_body(x_ref, c_ref, s1_ref, s2_ref, o_ref):
    t = _rope_pairs(x_ref[...].astype(F32), c_ref[...], s1_ref[...], s2_ref[...])
    o_ref[...] = t.astype(o_ref.dtype)


def _kr_rope(p, col_block, tables, rows):
    r = p.shape[0]
    tm = rows.tile(512)
    spec = pl.BlockSpec((tm, V7X_LANES), lambda i: (i, 0))
    return pl.pallas_call(
        _kr_rope_body,
        grid=(r // tm,),
        in_specs=[pl.BlockSpec((tm, V7X_LANES), lambda i: (i, col_block)), spec, spec, spec],
        out_specs=spec,
        out_shape=jax.ShapeDtypeStruct((r, V7X_LANES), BF16),
        compiler_params=_cparams(("arbitrary",), 16 * tm * V7X_LANES * 4),
    )(p, *tables)


ATTN_VT_ROWS = V7X_LANES + 16


def _attn_body(*refs, n_ctx, n_lat, tk):
    if n_lat:
        (q_ref, kc_ref, vc_ref, krc_ref, kl_ref, vl_ref, krl_ref,
         o_ref, kcat, vt_c, vt_l, acc_scr, s_a, s_b) = refs
    else:
        q_ref, kc_ref, vc_ref, krc_ref, o_ref, kcat, vt_c, acc_scr, s_a = refs
    ln = V7X_LANES
    nt = (((1,), (1,)), ((), ()))

    @pl.when(pl.program_id(2) == 0)
    def _():
        kcat[0:n_ctx, 0:ln] = kc_ref[...]
        kcat[0:n_ctx, ln:2 * ln] = krc_ref[...]
        eye = (lax.broadcasted_iota(jnp.int32, (ln, ln), 0)
               == lax.broadcasted_iota(jnp.int32, (ln, ln), 1)).astype(BF16)

        def ones_rows(n):
            first = lax.broadcasted_iota(jnp.int32, (ATTN_VT_ROWS - ln, n), 0) == 0
            return jnp.where(first, 1.0, 0.0).astype(BF16)

        vt_c[0:ln, :] = lax.dot_general(eye, vc_ref[...], nt, preferred_element_type=F32).astype(BF16)
        vt_c[ln:, :] = ones_rows(n_ctx)
        if n_lat:
            kcat[n_ctx:n_ctx + n_lat, 0:ln] = kl_ref[...]
            kcat[n_ctx:n_ctx + n_lat, ln:2 * ln] = krl_ref[...]
            for c in range(n_lat // tk):
                vt_l[c, 0:ln, :] = lax.dot_general(eye, vl_ref[c * tk:(c + 1) * tk, :], nt,
                                                   preferred_element_type=F32).astype(BF16)
                vt_l[c, ln:, :] = ones_rows(tk)

    q = q_ref[...]
    tq = q.shape[0]
    acc_scr[...] = jnp.zeros(acc_scr.shape, F32)

    def scores(k, dst):
        dst[0:k.shape[0], :] = lax.dot_general(k, q, nt, preferred_element_type=F32)

    def softmax_pv(src, n, v_t, m_prev):
        s = src[0:n, :]
        m_next = jnp.maximum(m_prev, jnp.max(s, axis=0, keepdims=True))
        p = jnp.exp2(s - m_next).astype(BF16)
        alpha = jnp.exp2(m_prev - m_next)
        acc_scr[...] = alpha * acc_scr[...] + jnp.dot(v_t, p, preferred_element_type=F32)
        return m_next

    def lat_keys(c):
        return kcat[pl.ds(pl.multiple_of(n_ctx + c * tk, ln), tk), :]

    m = jnp.full((1, tq), -jnp.inf, F32)
    scores(kcat[0:n_ctx, :], s_a)
    if not n_lat:
        m = softmax_pv(s_a, n_ctx, vt_c[...], m)
    else:
        n_ch = n_lat // tk
        scores(lat_keys(0), s_b)
        m = softmax_pv(s_a, n_ctx, vt_c[...], m)

        def pair(i, m):
            c = 2 * i
            scores(lat_keys(c + 1), s_a)
            m = softmax_pv(s_b, tk, vt_l[c], m)
            scores(lat_keys(c + 2), s_b)
            return softmax_pv(s_a, tk, vt_l[c + 1], m)

        m = lax.fori_loop(0, n_ch // 2 - 1, pair, m)
        scores(lat_keys(n_ch - 1), s_a)
        m = softmax_pv(s_b, tk, vt_l[n_ch - 2], m)
        m = softmax_pv(s_a, tk, vt_l[n_ch - 1], m)
    out = acc_scr[0:ln, :] / acc_scr[ln:ln + 1, :]
    o_ref[...] = out.T.astype(o_ref.dtype)


def _attention(q, kv, kr, rows, heads, latent):
    b, t, tc = rows.B, rows.T, rows.TC
    ln = V7X_LANES
    ctx_blk0 = rows.n_lat // tc
    if latent:
        tq = _pick(t, 1024, ln)
        nq = t // tq
        tk = _pick(t // 2, 1024, ln)
        assert tk >= tc
        n_lat = t
    else:
        tq, nq, tk, n_lat = tc, 1, tc, 0
    q_row = (lambda bb, qi: bb * nq + qi) if latent else (lambda bb, qi: ctx_blk0 * (tc // tq) + bb)
    in_specs = [
        pl.BlockSpec((tq, 2 * ln), lambda bb, h, qi: (q_row(bb, qi), h)),
        pl.BlockSpec((tc, ln), lambda bb, h, qi: (ctx_blk0 + bb, 2 * h)),
        pl.BlockSpec((tc, ln), lambda bb, h, qi: (ctx_blk0 + bb, 2 * h + 1)),
        pl.BlockSpec((tc, ln), lambda bb, h, qi: (ctx_blk0 + bb, 0)),
    ]
    args = [q, kv, kv, kr]
    scratch = [pltpu.VMEM((tc + n_lat, 2 * ln), BF16), pltpu.VMEM((ATTN_VT_ROWS, tc), BF16)]
    if latent:
        in_specs += [
            pl.BlockSpec((t, ln), lambda bb, h, qi: (bb, 2 * h)),
            pl.BlockSpec((t, ln), lambda bb, h, qi: (bb, 2 * h + 1)),
            pl.BlockSpec((t, ln), lambda bb, h, qi: (bb, 0)),
        ]
        args += [kv, kv, kr]
        scratch.append(pltpu.VMEM((n_lat // tk, ATTN_VT_ROWS, tk), BF16))
    scratch.append(pltpu.VMEM((ATTN_VT_ROWS, tq), F32))
    scratch += [pltpu.VMEM((tk, tq), F32)] * (2 if latent else 1)
    nk = tc + n_lat
    est = (2 * tq * 2 * ln * 2 + 6 * tc * ln * 2 + 6 * n_lat * ln * 2 + nk * 3 * ln * 2
           + 2 * tq * ln * 2 + 2 * tq * ln * 4 + 5 * tq * tk * 4)
    return pl.pallas_call(
        functools.partial(_attn_body, n_ctx=tc, n_lat=n_lat, tk=tk),
        grid=(b, heads, nq),
        in_specs=in_specs,
        out_specs=pl.BlockSpec((tq, ln), lambda bb, h, qi: (q_row(bb, qi), h)),
        out_shape=jax.ShapeDtypeStruct((rows.R, heads * ln), BF16),
        scratch_shapes=scratch,
        compiler_params=_cparams(("arbitrary", "arbitrary", "arbitrary"), est),
    )(*args)


def _scan_rows(a, bv, reverse):
    n = a.shape[0]
    row = lax.broadcasted_iota(jnp.int32, a.shape, 0)
    s = 1
    while s < n:
        if reverse:
            keep = row < (n - s)
            a_sh = jnp.where(keep, pltpu.roll(a, n - s, 0), 1.0)
            b_sh = jnp.where(keep, pltpu.roll(bv, n - s, 0), 0.0)
        else:
            keep = row >= s
            a_sh = jnp.where(keep, pltpu.roll(a, s, 0), 1.0)
            b_sh = jnp.where(keep, pltpu.roll(bv, s, 0), 0.0)
        bv = a * b_sh + bv
        a = a * a_sh
        s *= 2
    return a, bv


def _cumsum_rows(x, reverse):
    n = x.shape[0]
    row = lax.broadcasted_iota(jnp.int32, x.shape, 0)
    s = 1
    while s < n:
        if reverse:
            x = x + jnp.where(row < (n - s), pltpu.roll(x, n - s, 0), 0.0)
        else:
            x = x + jnp.where(row >= s, pltpu.roll(x, s, 0), 0.0)
        s *= 2
    return x


def _gelu_tanh(x):
    return 0.5 * x * (1.0 + jnp.tanh(math.sqrt(2.0 / math.pi) * (x + 0.044715 * (x * x * x))))


def _lru_body(xl_ref, xc_ref, gl_ref, gc_ref, cw_ref, cb_ref, wa_ref, ba_ref, wi_ref, bi_ref, lam_ref,
              yl_ref, yc_ref, cv_l, cv_c, hf_l, hf_c, *, seq, ctx, chunk, conv_chunk):
    pad = V7X_SUBLANES
    zeros8 = jnp.zeros((pad, V7X_LANES), F32)

    def conv_into(src_ref, dst, n, step):
        for c0 in range(0, n, step):
            lo = zeros8 if c0 == 0 else src_ref[c0 - pad:c0, :].astype(F32)
            hi = zeros8 if c0 + step >= n else src_ref[c0 + step:c0 + step + pad, :].astype(F32)
            ext = jnp.concatenate([lo, src_ref[c0:c0 + step, :].astype(F32), hi], axis=0)
            out = cb_ref[...]
            for kk in range(CONV_W):
                off = pad + kk - CONV_PAD_L
                out = out + ext[off:off + step, :] * cw_ref[kk:kk + 1, :]
            dst[c0:c0 + step, :] = out

    conv_into(xl_ref, cv_l, seq, conv_chunk)
    conv_into(xc_ref, cv_c, ctx, min(ctx, conv_chunk))

    for d, reverse in enumerate((False, True)):
        lam = lam_ref[d:d + 1, :]
        neg_sp = -LRU_C * (jnp.maximum(-lam, 0.0) + jnp.log(1.0 + jnp.exp(-jnp.abs(lam))))
        wa = wa_ref[d, 0]
        wi = wi_ref[d, 0]
        ba = ba_ref[d:d + 1, :]
        bi = bi_ref[d:d + 1, :]

        def block(x, carry):
            xg = x.astype(BF16)
            r = jax.nn.sigmoid(jnp.dot(xg, wa, preferred_element_type=F32) + ba)
            i = jax.nn.sigmoid(jnp.dot(xg, wi, preferred_element_type=F32) + bi)
            a = jnp.exp(neg_sp * r)
            bv = jnp.sqrt(1.0 - a * a) * (i * x)
            a_cum, h = _scan_rows(a, bv, reverse)
            h = a_cum * carry + h
            n = x.shape[0]
            new_carry = h[0:1, :] if reverse else h[n - 1:n, :]
            return h, new_carry

        def emit(dst_ref, fwd_scr, g_ref, start, n, h):
            if not reverse:
                fwd_scr[pl.ds(start, n), :] = h
            else:
                tot = fwd_scr[pl.ds(start, n), :] + h
                g = g_ref[pl.ds(start, n), :].astype(F32)
                dst_ref[pl.ds(start, n), :] = (tot * _gelu_tanh(g)).astype(dst_ref.dtype)

        carry = jnp.zeros((1, V7X_LANES), F32)
        cchunk = min(ctx, chunk)
        n_cc = ctx // cchunk
        order = range(n_cc - 1, -1, -1) if reverse else range(n_cc)
        lat_init = None
        for c in order:
            h, carry = block(cv_c[c * cchunk:(c + 1) * cchunk, :], carry)
            if lat_init is None:
                lat_init = h[cchunk - 1:cchunk, :] if reverse else h[0:1, :]
            emit(yc_ref, hf_c, gc_ref, c * cchunk, cchunk, h)
        carry = lat_init

        n_lc = seq // chunk

        def body(it, carry):
            c = (n_lc - 1 - it) if reverse else it
            start = pl.multiple_of(c * chunk, chunk)
            h, carry = block(cv_l[pl.ds(start, chunk), :], carry)
            emit(yl_ref, hf_l, gl_ref, start, chunk, h)
            return carry

        lax.fori_loop(0, n_lc, body, carry)


def _rglru(p, xb_col, gb_col, width, conv_w, conv_b, w_a, b_a, w_i, b_i, lam, rows):
    b, t, tc = rows.B, rows.T, rows.TC
    ln = V7X_LANES
    groups = width // ln
    ctx_blk0 = rows.n_lat // tc
    chunk = _pick(t, 256, V7X_SUBLANES)
    conv_chunk = _pick(t, 1024, V7X_SUBLANES)
    vec = lambda rws: pl.BlockSpec((rws, ln), lambda bb, g: (0, g))
    est = 2 * (t + tc) * ln * (2 + 2 + 2) + 2 * (t + tc) * ln * 4 + 64 * chunk * ln * 4
    return pl.pallas_call(
        functools.partial(_lru_body, seq=t, ctx=tc, chunk=chunk, conv_chunk=conv_chunk),
        grid=(b, groups),
        in_specs=[
            pl.BlockSpec((t, ln), lambda bb, g: (bb, xb_col + g)),
            pl.BlockSpec((tc, ln), lambda bb, g: (ctx_blk0 + bb, xb_col + g)),
            pl.BlockSpec((t, ln), lambda bb, g: (bb, gb_col + g)),
            pl.BlockSpec((tc, ln), lambda bb, g: (ctx_blk0 + bb, gb_col + g)),
            vec(CONV_W), vec(1),
            pl.BlockSpec((2, 1, ln, ln), lambda bb, g: (0, g, 0, 0)), vec(2),
            pl.BlockSpec((2, 1, ln, ln), lambda bb, g: (0, g, 0, 0)), vec(2),
            vec(2),
        ],
        out_specs=[
            pl.BlockSpec((t, ln), lambda bb, g: (bb, g)),
            pl.BlockSpec((tc, ln), lambda bb, g: (bb, g)),
        ],
        out_shape=[jax.ShapeDtypeStruct((rows.R, width), BF16),
                   jax.ShapeDtypeStruct((b * tc, width), BF16)],
        scratch_shapes=[pltpu.VMEM((t, ln), F32), pltpu.VMEM((tc, ln), F32),
                        pltpu.VMEM((t, ln), F32), pltpu.VMEM((tc, ln), F32)],
        compiler_params=_cparams(("arbitrary", "arbitrary"), est),
    )(p, p, p, p, conv_w, conv_b.reshape(1, width), w_a, b_a, w_i, b_i, lam)


def _log_sigmoid(x):
    return jnp.minimum(x, 0.0) - jnp.log(1.0 + jnp.exp(-jnp.abs(x)))


def _linattn_body(*refs, kind, reverse, tb, dk, dv):
    it = iter(refs)
    q_ref, k_ref, v_ref = next(it), next(it), next(it)
    if kind == "ret":
        cos_ref, sin_ref, dec_ref = next(it), next(it), next(it)
    else:
        ga_ref, wg_ref, bg_ref = next(it), next(it), next(it)
    if reverse:
        of_ref, g_ref, gain_ref = next(it), next(it), next(it)
    o_ref, st = next(it), next(it)
    hp = st.shape[0]

    @pl.when(pl.program_id(2) == 0)
    def _():
        st[...] = jnp.zeros(st.shape, F32)

    n_ch = tb // CHUNK
    ri = lax.broadcasted_iota(jnp.int32, (CHUNK, CHUNK), 0)
    ci = lax.broadcasted_iota(jnp.int32, (CHUNK, CHUNK), 1)
    mask = (ci >= ri) if reverse else (ci <= ri)
    nt = (((1,), (1,)), ((), ()))
    tn = (((0,), (0,)), ((), ()))
    pos = lax.broadcasted_iota(jnp.int32, (tb, 1), 0) % CHUNK
    n_terms = ((CHUNK - pos) if reverse else (pos + 1)).astype(F32)
    for hh in range(hp):
        q = q_ref[:, hh * dk:(hh + 1) * dk].astype(F32)
        k = k_ref[:, hh * dk:(hh + 1) * dk].astype(F32)
        if kind == "ret":
            c, s = cos_ref[...], sin_ref[...]
            hd = dk // 2
            q = jnp.concatenate([q[:, :hd] * c - q[:, hd:] * s, q[:, :hd] * s + q[:, hd:] * c], axis=1)
            k = jnp.concatenate([k[:, :hd] * c - k[:, hd:] * s, k[:, :hd] * s + k[:, hd:] * c], axis=1)
            bcum_all = n_terms * dec_ref[hh]
        else:
            z = jnp.dot(ga_ref[...], wg_ref[0, :, hh * dk:(hh + 1) * dk], preferred_element_type=F32)
            la = _log_sigmoid(z + bg_ref[0, :, hh * dk:(hh + 1) * dk]) * (1.0 / GLA_TAU)
        outs = [None] * n_ch
        for c in (range(n_ch - 1, -1, -1) if reverse else range(n_ch)):
            sl = slice(c * CHUNK, (c + 1) * CHUNK)
            bcum = bcum_all[sl] if kind == "ret" else _cumsum_rows(la[sl], reverse)
            bend = bcum[0:1, :] if reverse else bcum[CHUNK - 1:CHUNK, :]
            qe = (q[sl] * jnp.exp(bcum)).astype(BF16)
            ke = (k[sl] * jnp.exp(-bcum)).astype(BF16)
            ks = (k[sl] * jnp.exp(bend - bcum)).astype(BF16)
            vc = v_ref[sl, hh * dv:(hh + 1) * dv]
            att = lax.dot_general(qe, ke, nt, preferred_element_type=F32)
            att = jnp.where(mask, att, 0.0).astype(BF16)
            o = jnp.dot(att, vc, preferred_element_type=F32)
            o = o + lax.dot_general(qe, st[hh].astype(BF16), nt, preferred_element_type=F32)
            st[hh] = st[hh] * jnp.exp(bend) + lax.dot_general(vc, ks, tn, preferred_element_type=F32)
            outs[c] = o
        o = jnp.concatenate(outs, axis=0)
        cs = slice(hh * dv, (hh + 1) * dv)
        if not reverse:
            o_ref[:, cs] = o
        else:
            y = of_ref[:, cs] + o
            ms = jnp.mean(y * y, axis=-1, keepdims=True)
            y = (y * lax.rsqrt(ms + EPS)) * gain_ref[:, cs]
            g = g_ref[:, cs].astype(F32)
            o_ref[:, cs] = (y * (g * jax.nn.sigmoid(g))).astype(o_ref.dtype)


def _linattn(p, cols, heads, kind, reverse, rows, extra, o_fwd=None, gain=None):
    b, t, tc = rows.B, rows.T, rows.TC
    dk, dv = RET_DK, RET_DV
    tb = tc
    assert t % tb == 0 and tb % CHUNK == 0
    nt = 1 + t // tb
    ctx_blk0 = rows.n_lat // tb
    per_seq = t // tb

    def row_blk(bb, tt):
        lat = (per_seq - tt) if reverse else (tt - 1)
        return jnp.where(tt == 0, ctx_blk0 + bb, bb * per_seq + lat)

    qc, kc, vc, gc = cols
    hp = math.gcd(heads, LINATTN_HEADS_PER_STEP)
    wk, wv = hp * dk, hp * dv
    assert all(o % wk == 0 for o in (qc, kc)) and all(o % wv == 0 for o in (vc, gc))
    in_specs = [
        pl.BlockSpec((tb, wk), lambda bb, h, tt: (row_blk(bb, tt), qc // wk + h)),
        pl.BlockSpec((tb, wk), lambda bb, h, tt: (row_blk(bb, tt), kc // wk + h)),
        pl.BlockSpec((tb, wv), lambda bb, h, tt: (row_blk(bb, tt), vc // wv + h)),
    ]
    args = [p, p, p]
    if kind == "ret":
        cos, sin, dec = extra
        tab = pl.BlockSpec((tb, dk // 2), lambda bb, h, tt: (row_blk(bb, tt), 0))
        in_specs += [tab, tab, pl.BlockSpec((hp, 1, dk), lambda bb, h, tt: (h, 0, 0))]
        args += [cos, sin, dec]
    else:
        ga_col, wg, bg = extra
        in_specs += [
            pl.BlockSpec((tb, V7X_LANES), lambda bb, h, tt: (row_blk(bb, tt), ga_col // V7X_LANES)),
            pl.BlockSpec((1, V7X_LANES, wk), lambda bb, h, tt: (0, 0, h)),
            pl.BlockSpec((1, 1, wk), lambda bb, h, tt: (0, 0, h)),
        ]
        args += [p, wg, bg]
    if reverse:
        in_specs += [
            pl.BlockSpec((tb, wv), lambda bb, h, tt: (row_blk(bb, tt), h)),
            pl.BlockSpec((tb, wv), lambda bb, h, tt: (row_blk(bb, tt), gc // wv + h)),
            pl.BlockSpec((1, wv), lambda bb, h, tt: (0, h)),
        ]
        args += [o_fwd, p, gain]
    est = hp * (2 * tb * (2 * dk + dv) * 2 + 6 * tb * dv * 4 + 3 * dv * dk * 4 + 24 * tb * dk * 4 + 4 * tb * dv * 4)
    return pl.pallas_call(
        functools.partial(_linattn_body, kind=kind, reverse=reverse, tb=tb, dk=dk, dv=dv),
        grid=(b, heads // hp, nt),
        in_specs=in_specs,
        out_specs=pl.BlockSpec((tb, wv), lambda bb, h, tt: (row_blk(bb, tt), h)),
        out_shape=jax.ShapeDtypeStruct((rows.R, heads * dv), BF16 if reverse else F32),
        scratch_shapes=[pltpu.VMEM((hp, dv, dk), F32)],
        compiler_params=_cparams(("arbitrary", "arbitrary", "arbitrary"), est),
    )(*args)


def _pad_cols(w, n):
    return jnp.pad(w, ((0, 0), (0, n - w.shape[1])))


def _axial_angles(t, rot_dim):
    n_rows = t // GRID_W
    row = jnp.repeat(jnp.arange(n_rows, dtype=F32), GRID_W)
    col = jnp.tile(jnp.arange(GRID_W, dtype=F32), n_rows)
    n_freq = rot_dim // 4
    inv = ROPE_BASE ** (-jnp.arange(n_freq, dtype=F32) / n_freq)
    return jnp.concatenate([row[:, None] * inv, col[:, None] * inv], axis=-1)


def _row_tables(rows, rot_dim):
    ang = _axial_angles(rows.T, rot_dim)
    half = rot_dim // 2
    cos = jnp.concatenate([jnp.tile(jnp.cos(ang), (rows.B, 1)), jnp.ones((rows.n_ctx, half), F32)], axis=0)
    sin = jnp.concatenate([jnp.tile(jnp.sin(ang), (rows.B, 1)), jnp.zeros((rows.n_ctx, half), F32)], axis=0)
    return cos, sin


def _mla_tables(rows):
    cos, sin = _row_tables(rows, MLA_ROPE)
    half = MLA_ROPE // 2
    z = jnp.zeros((rows.R, V7X_LANES - MLA_ROPE), F32)
    zh = jnp.zeros((rows.R, half), F32)
    c = jnp.concatenate([cos, cos, z], axis=1)
    s1 = jnp.concatenate([-sin, zh, z], axis=1)
    s2 = jnp.concatenate([zh, sin, z], axis=1)
    return c, s1, s2


def _even_mixer(h, gain, mods, rows, n_out, w_in, q_norm, w_uq, kv_norm, w_ukv, conv_w, conv_b,
                w_a, b_a, w_i, b_i, lam, w_out, tables):
    q_rank, kv_rank, lru_w = q_norm.shape[0], kv_norm.shape[0], conv_w.shape[1]
    ln = V7X_LANES
    o_cq, o_ckv, o_kr, o_xb, o_gb = np.cumsum([0, q_rank, kv_rank, MLA_ROPE, lru_w]).tolist()
    w_cols = jnp.concatenate([w_in[:, :o_kr], w_in[:, o_xb:], w_in[:, o_kr:o_xb]], axis=1)
    n_real = w_cols.shape[1] + (ln - MLA_ROPE)
    n_pad = -(-n_real // (5 * V7X_MXU_DIM)) * (5 * V7X_MXU_DIM)
    p = _norm_matmul(h, gain, mods, 1, _pad_cols(w_cols, n_pad).astype(BF16), rows, 5 * V7X_MXU_DIM)
    c_cq, c_ckv, c_xb, c_gb, c_kr = 0, q_rank, q_rank + kv_rank, q_rank + kv_rank + lru_w, q_rank + kv_rank + 2 * lru_w

    scale = (MLA_NOPE + MLA_ROPE) ** -0.5 * math.log2(math.e)
    wq = (w_uq * scale).reshape(q_rank, MLA_HEADS, MLA_NOPE + MLA_ROPE)
    wq = jnp.pad(wq, ((0, 0), (0, 0), (0, 2 * ln - MLA_NOPE - MLA_ROPE))).reshape(q_rank, MLA_HEADS * 2 * ln)
    assert c_cq % q_rank == 0 and c_ckv % kv_rank == 0
    q = _lowrank_up(p, c_cq // q_rank, q_rank, q_norm, wq.astype(BF16), tables, rows, MLA_HEADS, True)
    kv = _lowrank_up(p, c_ckv // kv_rank, kv_rank, kv_norm, w_ukv.astype(BF16), tables, rows, MLA_HEADS, False)
    kr = _kr_rope(p, c_kr // ln, tables, rows)

    att = _attention(q, kv, kr, rows, MLA_HEADS, latent=True)
    att_c = _attention(q, kv, kr, rows, MLA_HEADS, latent=False)
    att = lax.dynamic_update_slice(att, att_c[rows.n_lat:], (rows.n_lat, 0))

    y_l, y_c = _rglru(p, c_xb // ln, c_gb // ln, lru_w, conv_w, conv_b,
                      w_a.astype(BF16), b_a, w_i.astype(BF16), b_i, lam, rows)
    lru = lax.dynamic_update_slice(y_l, y_c, (rows.n_lat, 0))
    n_att = MLA_HEADS * MLA_V
    assert MLA_HEADS * MLA_V == lru_w
    return _res_matmul([att, lru], w_out.astype(BF16), (), h, mods, 1, 1.0, rows, n_out)


def _odd_mixer(h, gain, mods, rows, n_out, w_in, ret_log_decay, ret_norm, gla_w_gate2, gla_b_gate, gla_norm,
               w_out, tables):
    ln = V7X_LANES
    sizes = [RET_HEADS * RET_DK, RET_HEADS * RET_DK, RET_HEADS * RET_DV, RET_HEADS * RET_DV,
             GLA_HEADS * GLA_DK, GLA_HEADS * GLA_DK, GLA_HEADS * GLA_DV, GLA_HEADS * GLA_DV]
    offs = np.cumsum([0] + sizes).tolist()
    rq, rk, rv, rg, gq, gk, gv, gr, ga = offs
    col_scale = jnp.ones((w_in.shape[1],), F32)
    col_scale = col_scale.at[rk:rv].set(RET_DK ** -0.5).at[gq:gk].set(GLA_DK ** -0.5)
    n_real = ga + ln
    n_pad = -(-n_real // (5 * V7X_MXU_DIM)) * (5 * V7X_MXU_DIM)
    p = _norm_matmul(h, gain, mods, 1, _pad_cols(w_in * col_scale, n_pad).astype(BF16), rows,
                     5 * V7X_MXU_DIM)

    cos, sin = tables
    ys = []
    for kind, heads, cols, norm in (("ret", RET_HEADS, (rq, rk, rv, rg), ret_norm),
                                    ("gla", GLA_HEADS, (gq, gk, gv, gr), gla_norm)):
        o_f = None
        for d, reverse in enumerate((False, True)):
            if kind == "ret":
                dec = jnp.broadcast_to(ret_log_decay[d][:, None, None], (heads, 1, RET_DK)).astype(F32)
                extra = (cos, sin, dec)
            else:
                wg = jnp.zeros((1, ln, heads * GLA_DK), F32).at[0, d * GLA_RANK:(d + 1) * GLA_RANK].set(gla_w_gate2[d])
                extra = (ga, wg.astype(BF16), gla_b_gate[d].reshape(1, 1, heads * GLA_DK))
            out = _linattn(p, cols, heads, kind, reverse, rows, extra, o_fwd=o_f,
                           gain=norm.reshape(1, heads * RET_DV))
            if reverse:
                ys.append(out)
            else:
                o_f = out
    n_ret = RET_HEADS * RET_DV
    assert RET_HEADS * RET_DV == GLA_HEADS * GLA_DV
    return _res_matmul(ys, w_out.astype(BF16), (), h, mods, 1, 1.0, rows, n_out)


def kernel(x, c, ctx, c_ctx, ada_w, ada_b, norm_w, ffn_w_gate, ffn_w_up, ffn_w_down, ev_w_in, mla_q_norm, mla_w_uq, mla_kv_norm, mla_w_ukv, lru_conv_w, lru_conv_b, lru_w_a, lru_b_a, lru_w_i, lru_b_i, lru_lambda, ev_w_out, od_w_in, ret_log_decay, ret_norm, gla_w_gate2, gla_b_gate, gla_norm, od_w_out, final_norm_w):
    batch, seq, d = x.shape
    tc = ctx.shape[1]
    depth = ada_w.shape[0]
    rows = _Rows(batch, seq, tc)
    h = (x.reshape(batch * seq, d), ctx.reshape(batch * tc, d))

    n_sets = 1 + batch
    c_rows = jnp.concatenate([c_ctx[None, :], c], axis=0)
    c_rows = jnp.pad(c_rows, ((0, -n_sets % V7X_SUBLANES), (0, 0)))
    mods_all = _ada(c_rows, ada_w, ada_b)[:, :n_sets].reshape(depth, n_sets, 9, d)

    mla_tabs = _mla_tables(rows)
    ret_tabs = _row_tables(rows, RET_DK)
    w_down_bf = ffn_w_down.astype(BF16)

    for l in range(depth):
        mods = mods_all[l]
        n_out = rows.n_lat if l == depth - 1 else rows.R

        def ffn(hh, k, idx, n_rows):
            u = _norm_mod(hh, norm_w[l, k], mods, k, rows, BF16, n_rows)
            a = _ffn_up(u, ffn_w_gate, ffn_w_up, (l, idx), rows)
            return _res_matmul([a], w_down_bf, (l, idx), hh, mods, k, FFN_RES, rows, n_rows)

        h = ffn(h, 0, 0, rows.R)
        if l % 2 == 0:
            e = l // 2
            h = _even_mixer(h, norm_w[l, 1], mods, rows, n_out, ev_w_in[e], mla_q_norm[e], mla_w_uq[e], mla_kv_norm[e],
                            mla_w_ukv[e], lru_conv_w[e], lru_conv_b[e], lru_w_a[e], lru_b_a[e],
                            lru_w_i[e], lru_b_i[e], lru_lambda[e], ev_w_out[e], mla_tabs)
        else:
            o = l // 2
            h = _odd_mixer(h, norm_w[l, 1], mods, rows, n_out, od_w_in[o], ret_log_decay[o], ret_norm[o], gla_w_gate2[o],
                           gla_b_gate[o], gla_norm[o], od_w_out[o], ret_tabs)
        h = ffn(h, 2, 1, n_out)

    out = _norm_mod(h, final_norm_w, mods_all[0], None, rows, F32, rows.n_lat)
    return out.reshape(batch, seq, d)
```

```python
import functools
import math

import jax
import jax.numpy as jnp
import numpy as np
from jax import lax
from jax.experimental import pallas as pl
from jax.experimental.pallas import tpu as pltpu

F32 = jnp.float32
BF16 = jnp.bfloat16

V7X_LANES = 128
V7X_SUBLANES = 8
V7X_MXU_DIM = 256
V7X_VMEM_BYTES = 64 * 1024 * 1024
V7X_VMEM_CEILING = 58 * 1024 * 1024

GRID_W = 64
EPS = 1e-6
ROPE_BASE = 10000.0
FFN_RES = 0.5
MLA_HEADS = 16
MLA_NOPE = 128
MLA_ROPE = 64
MLA_V = 128
LRU_BLOCKS = 16
LRU_C = 8.0
CONV_W = 4
CONV_PAD_L = 2
RET_HEADS = 4
RET_DK = 256
RET_DV = 512
GLA_HEADS = 4
GLA_DK = 256
GLA_DV = 512
GLA_RANK = 16
GLA_TAU = 16.0
CHUNK = 64
LINATTN_HEADS_PER_STEP = 4


def _cparams(sem, vmem_est):
    limit = int(min(V7X_VMEM_CEILING, max(vmem_est * 5 // 4 + (4 << 20), 16 << 20)))
    return pltpu.CompilerParams(dimension_semantics=sem, vmem_limit_bytes=limit)


def _pick(n, cap, mult):
    best = None
    for t in range(mult, min(n, cap) + 1, mult):
        if n % t == 0:
            best = t
    assert best is not None, (n, cap, mult)
    return best


class _Rows:
    def __init__(self, batch, seq, ctx):
        self.B, self.T, self.TC = batch, seq, ctx
        self.n_lat = batch * seq
        self.n_ctx = batch * ctx
        self.R = self.n_lat + self.n_ctx

    def tile(self, cap):
        g = math.gcd(self.T, self.n_ctx)
        return _pick(g, cap, V7X_SUBLANES)

    def set_of_tile(self, i, tm):
        n_lat_tiles = self.n_lat // tm
        per_seq = self.T // tm
        return jnp.where(i < n_lat_tiles, 1 + i // per_seq, 0)


def _ada_body(c_ref, w_ref, b_ref, o_ref):
    s = c_ref[...]
    s = (s * jax.nn.sigmoid(s)).astype(BF16)
    w = w_ref[0].astype(BF16)
    o_ref[0] = jnp.dot(s, w, preferred_element_type=F32) + b_ref[0]


def _ada(c_rows, ada_w, ada_b):
    depth, d, n = ada_w.shape
    rows = c_rows.shape[0]
    tn = _pick(n, 512, V7X_LANES)
    est = 2 * d * tn * 4 + d * tn * 2 + 4 * rows * d * 4
    return pl.pallas_call(
        _ada_body,
        grid=(depth, n // tn),
        in_specs=[
            pl.BlockSpec((rows, d), lambda l, j: (0, 0)),
            pl.BlockSpec((1, d, tn), lambda l, j: (l, 0, j)),
            pl.BlockSpec((1, 1, tn), lambda l, j: (l, 0, j)),
        ],
        out_specs=pl.BlockSpec((1, rows, tn), lambda l, j: (l, 0, j)),
        out_shape=jax.ShapeDtypeStruct((depth, rows, n), F32),
        compiler_params=_cparams(("arbitrary", "arbitrary"), est),
    )(c_rows, ada_w, ada_b.reshape(depth, 1, n))


def _h_specs(h, rows, tm, tn, tile_of):
    if not isinstance(h, tuple):
        return [pl.BlockSpec((tm, tn), tile_of)], [h]
    n_lat_tiles = rows.n_lat // tm

    def lat(*g):
        i, j = tile_of(*g)
        return jnp.minimum(i, n_lat_tiles - 1), j

    def ctx(*g):
        i, j = tile_of(*g)
        return jnp.maximum(i - n_lat_tiles, 0), j

    return [pl.BlockSpec((tm, tn), lat), pl.BlockSpec((tm, tn), ctx)], list(h)


def _read_h(h_refs, row_tile, n_lat_tiles):
    if len(h_refs) == 1:
        return h_refs[0][...]
    return jnp.where(row_tile < n_lat_tiles, h_refs[0][...], h_refs[1][...])


def _norm_body(*refs, k, n_h, n_lat_tiles):
    g_ref, m_ref, o_ref = refs[n_h:]
    x = _read_h(refs[:n_h], pl.program_id(0), n_lat_tiles)
    ms = jnp.mean(x * x, axis=-1, keepdims=True)
    y = (x * lax.rsqrt(ms + EPS)) * g_ref[...]
    if k is not None:
        y = y * (1.0 + m_ref[0, 3 * k + 1:3 * k + 2, :]) + m_ref[0, 3 * k:3 * k + 1, :]
    o_ref[...] = y.astype(o_ref.dtype)


def _norm_mod(h, gain, mods, k, rows, out_dtype, n_rows):
    d = gain.shape[0]
    tm = rows.tile(256)
    h_specs, h_args = _h_specs(h, rows, tm, d, lambda i: (i, 0))
    est = 2 * tm * d * (4 * len(h_args) + 4) + 2 * tm * d * 4
    return pl.pallas_call(
        functools.partial(_norm_body, k=k, n_h=len(h_args), n_lat_tiles=rows.n_lat // tm),
        grid=(n_rows // tm,),
        in_specs=h_specs + [
            pl.BlockSpec((1, d), lambda i: (0, 0)),
            pl.BlockSpec((1, 9, d), lambda i: (rows.set_of_tile(i, tm), 0, 0)),
        ],
        out_specs=pl.BlockSpec((tm, d), lambda i: (i, 0)),
        out_shape=jax.ShapeDtypeStruct((n_rows, d), out_dtype),
        compiler_params=_cparams(("arbitrary",), est),
    )(*h_args, gain.reshape(1, d), mods)


def _w_spec(w, lead, kk, tn, row_block=0):
    assert w.ndim == len(lead) + 2
    return pl.BlockSpec((None,) * len(lead) + (kk, tn), lambda j, i: tuple(lead) + (row_block, j))


NORM_ROWS_PER_PASS = 64


def _norm_into(u_scr, h_ref, g_ref, m_ref, gs_scr, k):
    rows, d = u_scr.shape
    ln, sub = V7X_LANES, V7X_SUBLANES
    step = min(NORM_ROWS_PER_PASS, rows)
    gs_scr[0] = jnp.broadcast_to(g_ref[...] * (1.0 + m_ref[0, 3 * k + 1:3 * k + 2, :]), (sub, d))
    gs_scr[1] = jnp.broadcast_to(m_ref[0, 3 * k:3 * k + 1, :], (sub, d))

    def rows_pass(it, carry):
        r0 = pl.multiple_of(it * step, step)
        acc = jnp.zeros((step, ln), F32)
        for c in range(0, d, ln):
            x = h_ref[pl.ds(r0, step), c:c + ln]
            acc = acc + x * x
        ms = jnp.sum(acc, axis=-1, keepdims=True) * (1.0 / d)
        rinv = jnp.broadcast_to(lax.rsqrt(ms + EPS), (step, ln))
        for c in range(0, d, ln):
            x = (h_ref[pl.ds(r0, step), c:c + ln] * rinv).reshape(step // sub, sub, ln)
            y = x * gs_scr[0, :, c:c + ln] + gs_scr[1, :, c:c + ln]
            u_scr[pl.ds(r0, step), c:c + ln] = y.reshape(step, ln).astype(u_scr.dtype)
        return carry

    lax.fori_loop(0, rows // step, rows_pass, 0)


def _mm_proj_body(h_ref, g_ref, m_ref, w_ref, ws_ref, o_ref, os_ref, u_scr, gs_scr, *, k):
    @pl.when(pl.program_id(1) == 0)
    def _():
        _norm_into(u_scr, h_ref, g_ref, m_ref, gs_scr, k)
        os_ref[...] = jnp.dot(u_scr[...], ws_ref[...], preferred_element_type=F32).astype(os_ref.dtype)

    o_ref[...] = jnp.dot(u_scr[...], w_ref[...], preferred_element_type=F32).astype(o_ref.dtype)


def _norm_matmul(h, gain, mods, k, w, w_side, rows, tn_cap):
    r, d = h.shape
    n = w.shape[-1]
    ns = w_side.shape[-1]
    tm = rows.tile(512)
    tn = _pick(n, tn_cap, V7X_LANES)
    est = (2 * tm * d * 4 + tm * d * 2 + 2 * d * (tn + ns) * 2 + 2 * tm * (tn + ns) * 2
           + 2 * tm * tn * 4 + 6 * NORM_ROWS_PER_PASS * d * 4)
    return pl.pallas_call(
        functools.partial(_mm_proj_body, k=k),
        grid=(r // tm, n // tn),
        in_specs=[
            pl.BlockSpec((tm, d), lambda i, j: (i, 0)),
            pl.BlockSpec((1, d), lambda i, j: (0, 0)),
            pl.BlockSpec((1, 9, d), lambda i, j: (rows.set_of_tile(i, tm), 0, 0)),
            pl.BlockSpec((d, tn), lambda i, j: (0, j)),
            pl.BlockSpec((d, ns), lambda i, j: (0, 0)),
        ],
        out_specs=[pl.BlockSpec((tm, tn), lambda i, j: (i, j)),
                   pl.BlockSpec((tm, ns), lambda i, j: (i, 0))],
        out_shape=[jax.ShapeDtypeStruct((r, n), BF16), jax.ShapeDtypeStruct((r, ns), BF16)],
        scratch_shapes=[pltpu.VMEM((tm, d), BF16), pltpu.VMEM((2, V7X_SUBLANES, d), F32)],
        compiler_params=_cparams(("arbitrary", "arbitrary"), est),
    )(h, gain.reshape(1, d), mods, w, w_side)


CAST_ROWS_PER_PASS = 512


def _mm_up_cast_body(x_ref, wg_ref, wu_ref, o_ref, wg_scr, wu_scr):
    @pl.when(pl.program_id(1) == 0)
    def _():
        kk = wg_scr.shape[0]
        step = min(CAST_ROWS_PER_PASS, kk)
        for r0 in range(0, kk, step):
            wg_scr[r0:r0 + step, :] = wg_ref[r0:r0 + step, :].astype(BF16)
            wu_scr[r0:r0 + step, :] = wu_ref[r0:r0 + step, :].astype(BF16)

    x = x_ref[...]
    g = jnp.dot(x, wg_scr[...], preferred_element_type=F32)
    u = jnp.dot(x, wu_scr[...], preferred_element_type=F32)
    o_ref[...] = (g * jax.nn.sigmoid(g) * u).astype(o_ref.dtype)


def _ffn_up(u, w_gate, w_up, lead, rows):
    r, k = u.shape
    n = w_gate.shape[-1]
    tm = rows.tile(512)
    tn = _pick(n, 512, V7X_LANES)
    est = 2 * tm * k * 2 + 4 * k * tn * 4 + 2 * k * tn * 2 + 2 * tm * tn * 2 + 2 * tm * tn * 4
    return pl.pallas_call(
        _mm_up_cast_body,
        grid=(n // tn, r // tm),
        in_specs=[
            pl.BlockSpec((tm, k), lambda j, i: (i, 0)),
            _w_spec(w_gate, lead, k, tn),
            _w_spec(w_up, lead, k, tn),
        ],
        out_specs=pl.BlockSpec((tm, tn), lambda j, i: (i, j)),
        out_shape=jax.ShapeDtypeStruct((r, n), BF16),
        scratch_shapes=[pltpu.VMEM((k, tn), BF16), pltpu.VMEM((k, tn), BF16)],
        compiler_params=_cparams(("arbitrary", "arbitrary"), est),
    )(u, w_gate, w_up)


def _mm_res_body(*refs, n_pairs, n_h, n_lat_tiles, coef, k):
    h_refs = refs[2 * n_pairs:2 * n_pairs + n_h]
    m_ref, o_ref = refs[2 * n_pairs + n_h:]
    acc = None
    for p in range(n_pairs):
        part = jnp.dot(refs[2 * p][...], refs[2 * p + 1][...], preferred_element_type=F32)
        acc = part if acc is None else acc + part
    gate = m_ref[0, 3 * k + 2:3 * k + 3, :]
    o_ref[...] = _read_h(h_refs, pl.program_id(1), n_lat_tiles) + (coef * gate) * acc


def _res_matmul(xs, w, lead, h, mods, k, coef, rows, n_rows):
    n = w.shape[-1]
    tm = rows.tile(512)
    tn = _pick(n, 512, V7X_LANES)
    in_specs, args, est = [], [], 0
    for p, x in enumerate(xs):
        kk = x.shape[1]
        assert kk * len(xs) == w.shape[-2]
        in_specs += [pl.BlockSpec((tm, kk), lambda j, i: (i, 0)), _w_spec(w, lead, kk, tn, p)]
        args += [x, w]
        est += 2 * tm * kk * 2 + 2 * kk * tn * 2
    h_specs, h_args = _h_specs(h, rows, tm, tn, lambda j, i: (i, j))
    in_specs += h_specs + [pl.BlockSpec((1, 9, tn), lambda j, i: (rows.set_of_tile(i, tm), 0, j))]
    est += (4 + 2 * len(h_args)) * tm * tn * 4
    return pl.pallas_call(
        functools.partial(_mm_res_body, n_pairs=len(xs), n_h=len(h_args),
                          n_lat_tiles=rows.n_lat // tm, coef=coef, k=k),
        grid=(n // tn, n_rows // tm),
        in_specs=in_specs,
        out_specs=pl.BlockSpec((tm, tn), lambda j, i: (i, j)),
        out_shape=jax.ShapeDtypeStruct((rows.R, n), F32),
        compiler_params=_cparams(("arbitrary", "arbitrary"), est),
    )(*args, *h_args, mods)


def _rope_pairs(t, c, s1, s2):
    half = MLA_ROPE // 2
    return t * c + pltpu.roll(t, V7X_LANES - half, 1) * s1 + pltpu.roll(t, half, 1) * s2


def _lowrank_body(x_ref, g_ref, w_ref, c_ref, s1_ref, s2_ref, o_ref, *, heads, rope):
    x = x_ref[...].astype(F32)
    ms = jnp.mean(x * x, axis=-1, keepdims=True)
    xn = ((x * lax.rsqrt(ms + EPS)) * g_ref[...]).astype(BF16)
    hw = 2 * V7X_LANES
    for h in range(heads):
        acc = jnp.dot(xn, w_ref[:, h * hw:(h + 1) * hw], preferred_element_type=F32)
        if rope:
            o_ref[:, h * hw:h * hw + V7X_LANES] = acc[:, :V7X_LANES].astype(o_ref.dtype)
            t = _rope_pairs(acc[:, V7X_LANES:], c_ref[...], s1_ref[...], s2_ref[...])
            o_ref[:, h * hw + V7X_LANES:(h + 1) * hw] = t.astype(o_ref.dtype)
        else:
            o_ref[:, h * hw:(h + 1) * hw] = acc.astype(o_ref.dtype)


def _lowrank_up(p, col_block, k, gain, w, tables, rows, heads, rope):
    r = p.shape[0]
    n = w.shape[1]
    tm = rows.tile(512)
    est = 2 * tm * k * 2 + 2 * k * n * 2 + 2 * tm * n * 2 + 6 * tm * V7X_LANES * 4 + tm * k * 8 + 4 * tm * 256 * 4
    tab_spec = pl.BlockSpec((tm, V7X_LANES), lambda i: (i, 0))
    return pl.pallas_call(
        functools.partial(_lowrank_body, heads=heads, rope=rope),
        grid=(r // tm,),
        in_specs=[
            pl.BlockSpec((tm, k), lambda i: (i, col_block)),
            pl.BlockSpec((1, k), lambda i: (0, 0)),
            pl.BlockSpec((k, n), lambda i: (0, 0)),
            tab_spec, tab_spec, tab_spec,
        ],
        out_specs=pl.BlockSpec((tm, n), lambda i: (i, 0)),
        out_shape=jax.ShapeDtypeStruct((r, n), BF16),
        compiler_params=_cparams(("arbitrary",), est),
    )(p, gain.reshape(1, k), w, *tables)


def _kr_rope_body(x_ref, c_ref, s1_ref, s2_ref, o_ref):
    t = _rope_pairs(x_ref[...].astype(F32), c_ref[...], s1_ref[...], s2_ref[...])
    o_ref[...] = t.astype(o_ref.dtype)


def _kr_rope(p, col_block, tables, rows):
    r = p.shape[0]
    tm = rows.tile(512)
    spec = pl.BlockSpec((tm, V7X_LANES), lambda i: (i, 0))
    return pl.pallas_call(
        _kr_rope_body,
        grid=(r // tm,),
        in_specs=[pl.BlockSpec((tm, V7X_LANES), lambda i: (i, col_block)), spec, spec, spec],
        out_specs=spec,
        out_shape=jax.ShapeDtypeStruct((r, V7X_LANES), BF16),
        compiler_params=_cparams(("arbitrary",), 16 * tm * V7X_LANES * 4),
    )(p, *tables)


ATTN_VT_ROWS = V7X_LANES + 16


def _attn_body(*refs, n_ctx, n_lat, tk):
    if n_lat:
        (q_ref, kc_ref, vc_ref, krc_ref, kl_ref, vl_ref, krl_ref,
         o_ref, kcat, vt_c, vt_l, acc_scr, s_a, s_b) = refs
    else:
        q_ref, kc_ref, vc_ref, krc_ref, o_ref, kcat, vt_c, acc_scr, s_a = refs
    ln = V7X_LANES
    nt = (((1,), (1,)), ((), ()))

    @pl.when(pl.program_id(2) == 0)
    def _():
        kcat[0:n_ctx, 0:ln] = kc_ref[...]
        kcat[0:n_ctx, ln:2 * ln] = krc_ref[...]
        eye = (lax.broadcasted_iota(jnp.int32, (ln, ln), 0)
               == lax.broadcasted_iota(jnp.int32, (ln, ln), 1)).astype(BF16)

        def ones_rows(n):
            first = lax.broadcasted_iota(jnp.int32, (ATTN_VT_ROWS - ln, n), 0) == 0
            return jnp.where(first, 1.0, 0.0).astype(BF16)

        vt_c[0:ln, :] = lax.dot_general(eye, vc_ref[...], nt, preferred_element_type=F32).astype(BF16)
        vt_c[ln:, :] = ones_rows(n_ctx)
        if n_lat:
            kcat[n_ctx:n_ctx + n_lat, 0:ln] = kl_ref[...]
            kcat[n_ctx:n_ctx + n_lat, ln:2 * ln] = krl_ref[...]
            for c in range(n_lat // tk):
                vt_l[c, 0:ln, :] = lax.dot_general(eye, vl_ref[c * tk:(c + 1) * tk, :], nt,
                                                   preferred_element_type=F32).astype(BF16)
                vt_l[c, ln:, :] = ones_rows(tk)

    q = q_ref[...]
    tq = q.shape[0]
    acc_scr[...] = jnp.zeros(acc_scr.shape, F32)

    def scores(k, dst):
        dst[0:k.shape[0], :] = lax.dot_general(k, q, nt, preferred_element_type=F32)

    def softmax_pv(src, n, v_t, m_prev):
        s = src[0:n, :]
        m_next = jnp.maximum(m_prev, jnp.max(s, axis=0, keepdims=True))
        p = jnp.exp2(s - m_next).astype(BF16)
        alpha = jnp.exp2(m_prev - m_next)
        acc_scr[...] = alpha * acc_scr[...] + jnp.dot(v_t, p, preferred_element_type=F32)
        return m_next

    def lat_keys(c):
        return kcat[pl.ds(pl.multiple_of(n_ctx + c * tk, ln), tk), :]

    m = jnp.full((1, tq), -jnp.inf, F32)
    scores(kcat[0:n_ctx, :], s_a)
    if not n_lat:
        m = softmax_pv(s_a, n_ctx, vt_c[...], m)
    else:
        n_ch = n_lat // tk
        scores(lat_keys(0), s_b)
        m = softmax_pv(s_a, n_ctx, vt_c[...], m)

        def pair(i, m):
            c = 2 * i
            scores(lat_keys(c + 1), s_a)
            m = softmax_pv(s_b, tk, vt_l[c], m)
            scores(lat_keys(c + 2), s_b)
            return softmax_pv(s_a, tk, vt_l[c + 1], m)

        m = lax.fori_loop(0, n_ch // 2 - 1, pair, m)
        scores(lat_keys(n_ch - 1), s_a)
        m = softmax_pv(s_b, tk, vt_l[n_ch - 2], m)
        m = softmax_pv(s_a, tk, vt_l[n_ch - 1], m)
    out = acc_scr[0:ln, :] / acc_scr[ln:ln + 1, :]
    o_ref[...] = out.T.astype(o_ref.dtype)


def _attention(q, kv, kr, rows, heads, latent):
    b, t, tc = rows.B, rows.T, rows.TC
    ln = V7X_LANES
    ctx_blk0 = rows.n_lat // tc
    if latent:
        tq = _pick(t, 1024, ln)
        nq = t // tq
        tk = _pick(t // 2, 1024, ln)
        assert tk >= tc
        n_lat = t
    else:
        tq, nq, tk, n_lat = tc, 1, tc, 0
    q_row = (lambda bb, qi: bb * nq + qi) if latent else (lambda bb, qi: ctx_blk0 * (tc // tq) + bb)
    in_specs = [
        pl.BlockSpec((tq, 2 * ln), lambda bb, h, qi: (q_row(bb, qi), h)),
        pl.BlockSpec((tc, ln), lambda bb, h, qi: (ctx_blk0 + bb, 2 * h)),
        pl.BlockSpec((tc, ln), lambda bb, h, qi: (ctx_blk0 + bb, 2 * h + 1)),
        pl.BlockSpec((tc, ln), lambda bb, h, qi: (ctx_blk0 + bb, 0)),
    ]
    args = [q, kv, kv, kr]
    scratch = [pltpu.VMEM((tc + n_lat, 2 * ln), BF16), pltpu.VMEM((ATTN_VT_ROWS, tc), BF16)]
    if latent:
        in_specs += [
            pl.BlockSpec((t, ln), lambda bb, h, qi: (bb, 2 * h)),
            pl.BlockSpec((t, ln), lambda bb, h, qi: (bb, 2 * h + 1)),
            pl.BlockSpec((t, ln), lambda bb, h, qi: (bb, 0)),
        ]
        args += [kv, kv, kr]
        scratch.append(pltpu.VMEM((n_lat // tk, ATTN_VT_ROWS, tk), BF16))
    scratch.append(pltpu.VMEM((ATTN_VT_ROWS, tq), F32))
    scratch += [pltpu.VMEM((tk, tq), F32)] * (2 if latent else 1)
    nk = tc + n_lat
    est = (2 * tq * 2 * ln * 2 + 6 * tc * ln * 2 + 6 * n_lat * ln * 2 + nk * 3 * ln * 2
           + 2 * tq * ln * 2 + 2 * tq * ln * 4 + 5 * tq * tk * 4)
    return pl.pallas_call(
        functools.partial(_attn_body, n_ctx=tc, n_lat=n_lat, tk=tk),
        grid=(b, heads, nq),
        in_specs=in_specs,
        out_specs=pl.BlockSpec((tq, ln), lambda bb, h, qi: (q_row(bb, qi), h)),
        out_shape=jax.ShapeDtypeStruct((rows.R, heads * ln), BF16),
        scratch_shapes=scratch,
        compiler_params=_cparams(("arbitrary", "arbitrary", "arbitrary"), est),
    )(*args)


def _scan_rows(a, bv, reverse):
    n = a.shape[0]
    row = lax.broadcasted_iota(jnp.int32, a.shape, 0)
    s = 1
    while s < n:
        if reverse:
            keep = row < (n - s)
            a_sh = jnp.where(keep, pltpu.roll(a, n - s, 0), 1.0)
            b_sh = jnp.where(keep, pltpu.roll(bv, n - s, 0), 0.0)
        else:
            keep = row >= s
            a_sh = jnp.where(keep, pltpu.roll(a, s, 0), 1.0)
            b_sh = jnp.where(keep, pltpu.roll(bv, s, 0), 0.0)
        bv = a * b_sh + bv
        a = a * a_sh
        s *= 2
    return a, bv


def _cumsum_rows(x, reverse):
    n = x.shape[0]
    row = lax.broadcasted_iota(jnp.int32, x.shape, 0)
    s = 1
    while s < n:
        if reverse:
            x = x + jnp.where(row < (n - s), pltpu.roll(x, n - s, 0), 0.0)
        else:
            x = x + jnp.where(row >= s, pltpu.roll(x, s, 0), 0.0)
        s *= 2
    return x


def _gelu_tanh(x):
    return 0.5 * x * (1.0 + jnp.tanh(math.sqrt(2.0 / math.pi) * (x + 0.044715 * (x * x * x))))


def _lru_body(xl_ref, xc_ref, gl_ref, gc_ref, cw_ref, cb_ref, wa_ref, ba_ref, wi_ref, bi_ref, lam_ref,
              yl_ref, yc_ref, cv_l, cv_c, hf_l, hf_c, *, seq, ctx, chunk, conv_chunk):
    pad = V7X_SUBLANES
    zeros8 = jnp.zeros((pad, V7X_LANES), F32)

    def conv_into(src_ref, dst, n, step):
        for c0 in range(0, n, step):
            lo = zeros8 if c0 == 0 else src_ref[c0 - pad:c0, :].astype(F32)
            hi = zeros8 if c0 + step >= n else src_ref[c0 + step:c0 + step + pad, :].astype(F32)
            ext = jnp.concatenate([lo, src_ref[c0:c0 + step, :].astype(F32), hi], axis=0)
            out = cb_ref[...]
            for kk in range(CONV_W):
                off = pad + kk - CONV_PAD_L
                out = out + ext[off:off + step, :] * cw_ref[kk:kk + 1, :]
            dst[c0:c0 + step, :] = out

    conv_into(xl_ref, cv_l, seq, conv_chunk)
    conv_into(xc_ref, cv_c, ctx, min(ctx, conv_chunk))

    for d, reverse in enumerate((False, True)):
        lam = lam_ref[d:d + 1, :]
        neg_sp = -LRU_C * (jnp.maximum(-lam, 0.0) + jnp.log(1.0 + jnp.exp(-jnp.abs(lam))))
        wa = wa_ref[d, 0]
        wi = wi_ref[d, 0]
        ba = ba_ref[d:d + 1, :]
        bi = bi_ref[d:d + 1, :]

        def block(x, carry):
            xg = x.astype(BF16)
            r = jax.nn.sigmoid(jnp.dot(xg, wa, preferred_element_type=F32) + ba)
            i = jax.nn.sigmoid(jnp.dot(xg, wi, preferred_element_type=F32) + bi)
            a = jnp.exp(neg_sp * r)
            bv = jnp.sqrt(1.0 - a * a) * (i * x)
            a_cum, h = _scan_rows(a, bv, reverse)
            h = a_cum * carry + h
            n = x.shape[0]
            new_carry = h[0:1, :] if reverse else h[n - 1:n, :]
            return h, new_carry

        def emit(dst_ref, fwd_scr, g_ref, start, n, h):
            if not reverse:
                fwd_scr[pl.ds(start, n), :] = h
            else:
                tot = fwd_scr[pl.ds(start, n), :] + h
                g = g_ref[pl.ds(start, n), :].astype(F32)
                dst_ref[pl.ds(start, n), :] = (tot * _gelu_tanh(g)).astype(dst_ref.dtype)

        carry = jnp.zeros((1, V7X_LANES), F32)
        cchunk = min(ctx, chunk)
        n_cc = ctx // cchunk
        order = range(n_cc - 1, -1, -1) if reverse else range(n_cc)
        lat_init = None
        for c in order:
            h, carry = block(cv_c[c * cchunk:(c + 1) * cchunk, :], carry)
            if lat_init is None:
                lat_init = h[cchunk - 1:cchunk, :] if reverse else h[0:1, :]
            emit(yc_ref, hf_c, gc_ref, c * cchunk, cchunk, h)
        carry = lat_init

        n_lc = seq // chunk

        def body(it, carry):
            c = (n_lc - 1 - it) if reverse else it
            start = pl.multiple_of(c * chunk, chunk)
            h, carry = block(cv_l[pl.ds(start, chunk), :], carry)
            emit(yl_ref, hf_l, gl_ref, start, chunk, h)
            return carry

        lax.fori_loop(0, n_lc, body, carry)


def _rglru(p, xb_col, gb_col, width, conv_w, conv_b, w_a, b_a, w_i, b_i, lam, rows):
    b, t, tc = rows.B, rows.T, rows.TC
    ln = V7X_LANES
    groups = width // ln
    ctx_blk0 = rows.n_lat // tc
    chunk = _pick(t, 256, V7X_SUBLANES)
    conv_chunk = _pick(t, 1024, V7X_SUBLANES)
    vec = lambda rws: pl.BlockSpec((rws, ln), lambda bb, g: (0, g))
    est = 2 * (t + tc) * ln * (2 + 2 + 2) + 2 * (t + tc) * ln * 4 + 64 * chunk * ln * 4
    return pl.pallas_call(
        functools.partial(_lru_body, seq=t, ctx=tc, chunk=chunk, conv_chunk=conv_chunk),
        grid=(b, groups),
        in_specs=[
            pl.BlockSpec((t, ln), lambda bb, g: (bb, xb_col + g)),
            pl.BlockSpec((tc, ln), lambda bb, g: (ctx_blk0 + bb, xb_col + g)),
            pl.BlockSpec((t, ln), lambda bb, g: (bb, gb_col + g)),
            pl.BlockSpec((tc, ln), lambda bb, g: (ctx_blk0 + bb, gb_col + g)),
            vec(CONV_W), vec(1),
            pl.BlockSpec((2, 1, ln, ln), lambda bb, g: (0, g, 0, 0)), vec(2),
            pl.BlockSpec((2, 1, ln, ln), lambda bb, g: (0, g, 0, 0)), vec(2),
            vec(2),
        ],
        out_specs=[
            pl.BlockSpec((t, ln), lambda bb, g: (bb, g)),
            pl.BlockSpec((tc, ln), lambda bb, g: (bb, g)),
        ],
        out_shape=[jax.ShapeDtypeStruct((rows.R, width), BF16),
                   jax.ShapeDtypeStruct((b * tc, width), BF16)],
        scratch_shapes=[pltpu.VMEM((t, ln), F32), pltpu.VMEM((tc, ln), F32),
                        pltpu.VMEM((t, ln), F32), pltpu.VMEM((tc, ln), F32)],
        compiler_params=_cparams(("arbitrary", "arbitrary"), est),
    )(p, p, p, p, conv_w, conv_b.reshape(1, width), w_a, b_a, w_i, b_i, lam)


def _log_sigmoid(x):
    return jnp.minimum(x, 0.0) - jnp.log(1.0 + jnp.exp(-jnp.abs(x)))


def _linattn_body(*refs, kind, reverse, tb, dk, dv):
    it = iter(refs)
    q_ref, k_ref, v_ref = next(it), next(it), next(it)
    if kind == "ret":
        cos_ref, sin_ref, dec_ref = next(it), next(it), next(it)
    else:
        ga_ref, wg_ref, bg_ref = next(it), next(it), next(it)
    if reverse:
        of_ref, g_ref, gain_ref = next(it), next(it), next(it)
    o_ref, st = next(it), next(it)
    hp = st.shape[0]

    @pl.when(pl.program_id(2) == 0)
    def _():
        st[...] = jnp.zeros(st.shape, F32)

    n_ch = tb // CHUNK
    ri = lax.broadcasted_iota(jnp.int32, (CHUNK, CHUNK), 0)
    ci = lax.broadcasted_iota(jnp.int32, (CHUNK, CHUNK), 1)
    mask = (ci >= ri) if reverse else (ci <= ri)
    nt = (((1,), (1,)), ((), ()))
    tn = (((0,), (0,)), ((), ()))
    pos = lax.broadcasted_iota(jnp.int32, (tb, 1), 0) % CHUNK
    n_terms = ((CHUNK - pos) if reverse else (pos + 1)).astype(F32)
    for hh in range(hp):
        q = q_ref[:, hh * dk:(hh + 1) * dk].astype(F32)
        k = k_ref[:, hh * dk:(hh + 1) * dk].astype(F32)
        if kind == "ret":
            c, s = cos_ref[...], sin_ref[...]
            hd = dk // 2
            q = jnp.concatenate([q[:, :hd] * c - q[:, hd:] * s, q[:, :hd] * s + q[:, hd:] * c], axis=1)
            k = jnp.concatenate([k[:, :hd] * c - k[:, hd:] * s, k[:, :hd] * s + k[:, hd:] * c], axis=1)
            bcum_all = n_terms * dec_ref[hh]
        else:
            z = jnp.dot(ga_ref[...], wg_ref[0, :, hh * dk:(hh + 1) * dk], preferred_element_type=F32)
            la = _log_sigmoid(z + bg_ref[0, :, hh * dk:(hh + 1) * dk]) * (1.0 / GLA_TAU)
        outs = [None] * n_ch
        for c in (range(n_ch - 1, -1, -1) if reverse else range(n_ch)):
            sl = slice(c * CHUNK, (c + 1) * CHUNK)
            bcum = bcum_all[sl] if kind == "ret" else _cumsum_rows(la[sl], reverse)
            bend = bcum[0:1, :] if reverse else bcum[CHUNK - 1:CHUNK, :]
            qe = (q[sl] * jnp.exp(bcum)).astype(BF16)
            ke = (k[sl] * jnp.exp(-bcum)).astype(BF16)
            ks = (k[sl] * jnp.exp(bend - bcum)).astype(BF16)
            vc = v_ref[sl, hh * dv:(hh + 1) * dv]
            att = lax.dot_general(qe, ke, nt, preferred_element_type=F32)
            att = jnp.where(mask, att, 0.0).astype(BF16)
            o = jnp.dot(att, vc, preferred_element_type=F32)
            o = o + lax.dot_general(qe, st[hh].astype(BF16), nt, preferred_element_type=F32)
            st[hh] = st[hh] * jnp.exp(bend) + lax.dot_general(vc, ks, tn, preferred_element_type=F32)
            outs[c] = o
        o = jnp.concatenate(outs, axis=0)
        cs = slice(hh * dv, (hh + 1) * dv)
        if not reverse:
            o_ref[:, cs] = o
        else:
            y = of_ref[:, cs] + o
            ms = jnp.mean(y * y, axis=-1, keepdims=True)
            y = (y * lax.rsqrt(ms + EPS)) * gain_ref[:, cs]
            g = g_ref[:, cs].astype(F32)
            o_ref[:, cs] = (y * (g * jax.nn.sigmoid(g))).astype(o_ref.dtype)


def _linattn(p, cols, heads, kind, reverse, rows, extra, o_fwd=None, gain=None):
    b, t, tc = rows.B, rows.T, rows.TC
    dk, dv = RET_DK, RET_DV
    tb = tc
    assert t % tb == 0 and tb % CHUNK == 0
    nt = 1 + t // tb
    ctx_blk0 = rows.n_lat // tb
    per_seq = t // tb

    def row_blk(bb, tt):
        lat = (per_seq - tt) if reverse else (tt - 1)
        return jnp.where(tt == 0, ctx_blk0 + bb, bb * per_seq + lat)

    qc, kc, vc, gc = cols
    hp = math.gcd(heads, LINATTN_HEADS_PER_STEP)
    wk, wv = hp * dk, hp * dv
    assert all(o % wk == 0 for o in (qc, kc)) and all(o % wv == 0 for o in (vc, gc))
    in_specs = [
        pl.BlockSpec((tb, wk), lambda bb, h, tt: (row_blk(bb, tt), qc // wk + h)),
        pl.BlockSpec((tb, wk), lambda bb, h, tt: (row_blk(bb, tt), kc // wk + h)),
        pl.BlockSpec((tb, wv), lambda bb, h, tt: (row_blk(bb, tt), vc // wv + h)),
    ]
    args = [p, p, p]
    if kind == "ret":
        cos, sin, dec = extra
        tab = pl.BlockSpec((tb, dk // 2), lambda bb, h, tt: (row_blk(bb, tt), 0))
        in_specs += [tab, tab, pl.BlockSpec((hp, 1, dk), lambda bb, h, tt: (h, 0, 0))]
        args += [cos, sin, dec]
    else:
        ga_rows, wg, bg = extra
        in_specs += [
            pl.BlockSpec((tb, V7X_LANES), lambda bb, h, tt: (row_blk(bb, tt), 0)),
            pl.BlockSpec((1, V7X_LANES, wk), lambda bb, h, tt: (0, 0, h)),
            pl.BlockSpec((1, 1, wk), lambda bb, h, tt: (0, 0, h)),
        ]
        args += [ga_rows, wg, bg]
    if reverse:
        in_specs += [
            pl.BlockSpec((tb, wv), lambda bb, h, tt: (row_blk(bb, tt), h)),
            pl.BlockSpec((tb, wv), lambda bb, h, tt: (row_blk(bb, tt), gc // wv + h)),
            pl.BlockSpec((1, wv), lambda bb, h, tt: (0, h)),
        ]
        args += [o_fwd, p, gain]
    est = hp * (2 * tb * (2 * dk + dv) * 2 + 6 * tb * dv * 4 + 3 * dv * dk * 4 + 24 * tb * dk * 4 + 4 * tb * dv * 4)
    return pl.pallas_call(
        functools.partial(_linattn_body, kind=kind, reverse=reverse, tb=tb, dk=dk, dv=dv),
        grid=(b, heads // hp, nt),
        in_specs=in_specs,
        out_specs=pl.BlockSpec((tb, wv), lambda bb, h, tt: (row_blk(bb, tt), h)),
        out_shape=jax.ShapeDtypeStruct((rows.R, heads * dv), BF16 if reverse else F32),
        scratch_shapes=[pltpu.VMEM((hp, dv, dk), F32)],
        compiler_params=_cparams(("arbitrary", "arbitrary", "arbitrary"), est),
    )(*args)


def _pad_cols(w, n):
    return jnp.pad(w, ((0, 0), (0, n - w.shape[1])))


def _axial_angles(t, rot_dim):
    n_rows = t // GRID_W
    row = jnp.repeat(jnp.arange(n_rows, dtype=F32), GRID_W)
    col = jnp.tile(jnp.arange(GRID_W, dtype=F32), n_rows)
    n_freq = rot_dim // 4
    inv = ROPE_BASE ** (-jnp.arange(n_freq, dtype=F32) / n_freq)
    return jnp.concatenate([row[:, None] * inv, col[:, None] * inv], axis=-1)


def _row_tables(rows, rot_dim):
    ang = _axial_angles(rows.T, rot_dim)
    half = rot_dim // 2
    cos = jnp.concatenate([jnp.tile(jnp.cos(ang), (rows.B, 1)), jnp.ones((rows.n_ctx, half), F32)], axis=0)
    sin = jnp.concatenate([jnp.tile(jnp.sin(ang), (rows.B, 1)), jnp.zeros((rows.n_ctx, half), F32)], axis=0)
    return cos, sin


def _mla_tables(rows):
    cos, sin = _row_tables(rows, MLA_ROPE)
    half = MLA_ROPE // 2
    z = jnp.zeros((rows.R, V7X_LANES - MLA_ROPE), F32)
    zh = jnp.zeros((rows.R, half), F32)
    c = jnp.concatenate([cos, cos, z], axis=1)
    s1 = jnp.concatenate([-sin, zh, z], axis=1)
    s2 = jnp.concatenate([zh, sin, z], axis=1)
    return c, s1, s2


def _even_mixer(h, gain, mods, rows, n_out, w_in, q_norm, w_uq, kv_norm, w_ukv, conv_w, conv_b,
                w_a, b_a, w_i, b_i, lam, w_out, tables):
    q_rank, kv_rank, lru_w = q_norm.shape[0], kv_norm.shape[0], conv_w.shape[1]
    ln = V7X_LANES
    o_cq, o_ckv, o_kr, o_xb, o_gb = np.cumsum([0, q_rank, kv_rank, MLA_ROPE, lru_w]).tolist()
    w_main = jnp.concatenate([w_in[:, :o_kr], w_in[:, o_xb:]], axis=1).astype(BF16)
    w_kr = _pad_cols(w_in[:, o_kr:o_xb], ln).astype(BF16)
    p, kr_raw = _norm_matmul(h, gain, mods, 1, w_main, w_kr, rows, 4 * V7X_MXU_DIM)
    c_cq, c_ckv, c_xb, c_gb = 0, q_rank, q_rank + kv_rank, q_rank + kv_rank + lru_w

    scale = (MLA_NOPE + MLA_ROPE) ** -0.5 * math.log2(math.e)
    wq = (w_uq * scale).reshape(q_rank, MLA_HEADS, MLA_NOPE + MLA_ROPE)
    wq = jnp.pad(wq, ((0, 0), (0, 0), (0, 2 * ln - MLA_NOPE - MLA_ROPE))).reshape(q_rank, MLA_HEADS * 2 * ln)
    assert c_cq % q_rank == 0 and c_ckv % kv_rank == 0
    q = _lowrank_up(p, c_cq // q_rank, q_rank, q_norm, wq.astype(BF16), tables, rows, MLA_HEADS, True)
    kv = _lowrank_up(p, c_ckv // kv_rank, kv_rank, kv_norm, w_ukv.astype(BF16), tables, rows, MLA_HEADS, False)
    kr = _kr_rope(kr_raw, 0, tables, rows)

    att = _attention(q, kv, kr, rows, MLA_HEADS, latent=True)
    att_c = _attention(q, kv, kr, rows, MLA_HEADS, latent=False)
    att = lax.dynamic_update_slice(att, att_c[rows.n_lat:], (rows.n_lat, 0))

    y_l, y_c = _rglru(p, c_xb // ln, c_gb // ln, lru_w, conv_w, conv_b,
                      w_a.astype(BF16), b_a, w_i.astype(BF16), b_i, lam, rows)
    lru = lax.dynamic_update_slice(y_l, y_c, (rows.n_lat, 0))
    n_att = MLA_HEADS * MLA_V
    assert MLA_HEADS * MLA_V == lru_w
    return _res_matmul([att, lru], w_out.astype(BF16), (), h, mods, 1, 1.0, rows, n_out)


def _odd_mixer(h, gain, mods, rows, n_out, w_in, ret_log_decay, ret_norm, gla_w_gate2, gla_b_gate, gla_norm,
               w_out, tables):
    ln = V7X_LANES
    sizes = [RET_HEADS * RET_DK, RET_HEADS * RET_DK, RET_HEADS * RET_DV, RET_HEADS * RET_DV,
             GLA_HEADS * GLA_DK, GLA_HEADS * GLA_DK, GLA_HEADS * GLA_DV, GLA_HEADS * GLA_DV]
    offs = np.cumsum([0] + sizes).tolist()
    rq, rk, rv, rg, gq, gk, gv, gr, ga = offs
    col_scale = jnp.ones((ga,), F32)
    col_scale = col_scale.at[rk:rv].set(RET_DK ** -0.5).at[gq:gk].set(GLA_DK ** -0.5)
    w_main = (w_in[:, :ga] * col_scale).astype(BF16)
    w_ga = _pad_cols(w_in[:, ga:], ln).astype(BF16)
    p, ga_rows = _norm_matmul(h, gain, mods, 1, w_main, w_ga, rows, 4 * V7X_MXU_DIM)

    cos, sin = tables
    ys = []
    for kind, heads, cols, norm in (("ret", RET_HEADS, (rq, rk, rv, rg), ret_norm),
                                    ("gla", GLA_HEADS, (gq, gk, gv, gr), gla_norm)):
        o_f = None
        for d, reverse in enumerate((False, True)):
            if kind == "ret":
                dec = jnp.broadcast_to(ret_log_decay[d][:, None, None], (heads, 1, RET_DK)).astype(F32)
                extra = (cos, sin, dec)
            else:
                wg = jnp.zeros((1, ln, heads * GLA_DK), F32).at[0, d * GLA_RANK:(d + 1) * GLA_RANK].set(gla_w_gate2[d])
                extra = (ga_rows, wg.astype(BF16), gla_b_gate[d].reshape(1, 1, heads * GLA_DK))
            out = _linattn(p, cols, heads, kind, reverse, rows, extra, o_fwd=o_f,
                           gain=norm.reshape(1, heads * RET_DV))
            if reverse:
                ys.append(out)
            else:
                o_f = out
    n_ret = RET_HEADS * RET_DV
    assert RET_HEADS * RET_DV == GLA_HEADS * GLA_DV
    return _res_matmul(ys, w_out.astype(BF16), (), h, mods, 1, 1.0, rows, n_out)


def kernel(x, c, ctx, c_ctx, ada_w, ada_b, norm_w, ffn_w_gate, ffn_w_up, ffn_w_down, ev_w_in, mla_q_norm, mla_w_uq, mla_kv_norm, mla_w_ukv, lru_conv_w, lru_conv_b, lru_w_a, lru_b_a, lru_w_i, lru_b_i, lru_lambda, ev_w_out, od_w_in, ret_log_decay, ret_norm, gla_w_gate2, gla_b_gate, gla_norm, od_w_out, final_norm_w):
    batch, seq, d = x.shape
    tc = ctx.shape[1]
    depth = ada_w.shape[0]
    rows = _Rows(batch, seq, tc)
    h = (x.reshape(batch * seq, d), ctx.reshape(batch * tc, d))

    n_sets = 1 + batch
    c_rows = jnp.concatenate([c_ctx[None, :], c], axis=0)
    c_rows = jnp.pad(c_rows, ((0, -n_sets % V7X_SUBLANES), (0, 0)))
    mods_all = _ada(c_rows, ada_w, ada_b)[:, :n_sets].reshape(depth, n_sets, 9, d)

    mla_tabs = _mla_tables(rows)
    ret_tabs = _row_tables(rows, RET_DK)
    w_down_bf = ffn_w_down.astype(BF16)

    for l in range(depth):
        mods = mods_all[l]
        n_out = rows.n_lat if l == depth - 1 else rows.R

        def ffn(hh, k, idx, n_rows):
            u = _norm_mod(hh, norm_w[l, k], mods, k, rows, BF16, n_rows)
            a = _ffn_up(u, ffn_w_gate, ffn_w_up, (l, idx), rows)
            return _res_matmul([a], w_down_bf, (l, idx), hh, mods, k, FFN_RES, rows, n_rows)

        h = ffn(h, 0, 0, rows.R)
        if l % 2 == 0:
            e = l // 2
            h = _even_mixer(h, norm_w[l, 1], mods, rows, n_out, ev_w_in[e], mla_q_norm[e], mla_w_uq[e], mla_kv_norm[e],
                            mla_w_ukv[e], lru_conv_w[e], lru_conv_b[e], lru_w_a[e], lru_b_a[e],
                            lru_w_i[e], lru_b_i[e], lru_lambda[e], ev_w_out[e], mla_tabs)
        else:
            o = l // 2
            h = _odd_mixer(h, norm_w[l, 1], mods, rows, n_out, od_w_in[o], ret_log_decay[o], ret_norm[o], gla_w_gate2[o],
                           gla_b_gate[o], gla_norm[o], od_w_out[o], ret_tabs)
        h = ffn(h, 2, 1, n_out)

    out = _norm_mod(h, final_norm_w, mods_all[0], None, rows, F32, rows.n_lat)
    return out.reshape(batch, seq, d)
```

```python
import functools
import math

import jax
import jax.numpy as jnp
import numpy as np
from jax import lax
from jax.experimental import pallas as pl
from jax.experimental.pallas import tpu as pltpu

F32 = jnp.float32
BF16 = jnp.bfloat16

V7X_LANES = 128
V7X_SUBLANES = 8
V7X_MXU_DIM = 256
V7X_VMEM_BYTES = 64 * 1024 * 1024
V7X_VMEM_CEILING = 58 * 1024 * 1024

GRID_W = 64
EPS = 1e-6
ROPE_BASE = 10000.0
FFN_RES = 0.5
MLA_HEADS = 16
MLA_NOPE = 128
MLA_ROPE = 64
MLA_V = 128
LRU_BLOCKS = 16
LRU_C = 8.0
CONV_W = 4
CONV_PAD_L = 2
RET_HEADS = 4
RET_DK = 256
RET_DV = 512
GLA_HEADS = 4
GLA_DK = 256
GLA_DV = 512
GLA_RANK = 16
GLA_TAU = 16.0
CHUNK = 64
LINATTN_HEADS_PER_STEP = 4


def _cparams(sem, vmem_est):
    limit = int(min(V7X_VMEM_CEILING, max(vmem_est * 5 // 4 + (4 << 20), 16 << 20)))
    return pltpu.CompilerParams(dimension_semantics=sem, vmem_limit_bytes=limit)


def _pick(n, cap, mult):
    best = None
    for t in range(mult, min(n, cap) + 1, mult):
        if n % t == 0:
            best = t
    assert best is not None, (n, cap, mult)
    return best


class _Rows:
    def __init__(self, batch, seq, ctx):
        self.B, self.T, self.TC = batch, seq, ctx
        self.n_lat = batch * seq
        self.n_ctx = batch * ctx
        self.R = self.n_lat + self.n_ctx

    def tile(self, cap):
        g = math.gcd(self.T, self.n_ctx)
        return _pick(g, cap, V7X_SUBLANES)

    def set_of_tile(self, i, tm):
        n_lat_tiles = self.n_lat // tm
        per_seq = self.T // tm
        return jnp.where(i < n_lat_tiles, 1 + i // per_seq, 0)


def _ada_body(c_ref, w_ref, b_ref, o_ref):
    s = c_ref[...]
    s = (s * jax.nn.sigmoid(s)).astype(BF16)
    w = w_ref[0].astype(BF16)
    o_ref[0] = jnp.dot(s, w, preferred_element_type=F32) + b_ref[0]


def _ada(c_rows, ada_w, ada_b):
    depth, d, n = ada_w.shape
    rows = c_rows.shape[0]
    tn = _pick(n, 512, V7X_LANES)
    est = 2 * d * tn * 4 + d * tn * 2 + 4 * rows * d * 4
    return pl.pallas_call(
        _ada_body,
        grid=(depth, n // tn),
        in_specs=[
            pl.BlockSpec((rows, d), lambda l, j: (0, 0)),
            pl.BlockSpec((1, d, tn), lambda l, j: (l, 0, j)),
            pl.BlockSpec((1, 1, tn), lambda l, j: (l, 0, j)),
        ],
        out_specs=pl.BlockSpec((1, rows, tn), lambda l, j: (l, 0, j)),
        out_shape=jax.ShapeDtypeStruct((depth, rows, n), F32),
        compiler_params=_cparams(("arbitrary", "arbitrary"), est),
    )(c_rows, ada_w, ada_b.reshape(depth, 1, n))


def _h_specs(h, rows, tm, tn, tile_of):
    if not isinstance(h, tuple):
        return [pl.BlockSpec((tm, tn), tile_of)], [h]
    n_lat_tiles = rows.n_lat // tm

    def lat(*g):
        i, j = tile_of(*g)
        return jnp.minimum(i, n_lat_tiles - 1), j

    def ctx(*g):
        i, j = tile_of(*g)
        return jnp.maximum(i - n_lat_tiles, 0), j

    return [pl.BlockSpec((tm, tn), lat), pl.BlockSpec((tm, tn), ctx)], list(h)


def _read_h(h_refs, row_tile, n_lat_tiles):
    if len(h_refs) == 1:
        return h_refs[0][...]
    return jnp.where(row_tile < n_lat_tiles, h_refs[0][...], h_refs[1][...])


def _norm_body(*refs, k, n_h, n_lat_tiles):
    g_ref, m_ref, o_ref = refs[n_h:]
    x = _read_h(refs[:n_h], pl.program_id(0), n_lat_tiles)
    ms = jnp.mean(x * x, axis=-1, keepdims=True)
    y = (x * lax.rsqrt(ms + EPS)) * g_ref[...]
    if k is not None:
        y = y * (1.0 + m_ref[0, 3 * k + 1:3 * k + 2, :]) + m_ref[0, 3 * k:3 * k + 1, :]
    o_ref[...] = y.astype(o_ref.dtype)


def _norm_mod(h, gain, mods, k, rows, out_dtype, n_rows):
    d = gain.shape[0]
    tm = rows.tile(256 if isinstance(h, tuple) else 512)
    h_specs, h_args = _h_specs(h, rows, tm, d, lambda i: (i, 0))
    est = 2 * tm * d * (4 * len(h_args) + 4) + 2 * tm * d * 4
    return pl.pallas_call(
        functools.partial(_norm_body, k=k, n_h=len(h_args), n_lat_tiles=rows.n_lat // tm),
        grid=(n_rows // tm,),
        in_specs=h_specs + [
            pl.BlockSpec((1, d), lambda i: (0, 0)),
            pl.BlockSpec((1, 9, d), lambda i: (rows.set_of_tile(i, tm), 0, 0)),
        ],
        out_specs=pl.BlockSpec((tm, d), lambda i: (i, 0)),
        out_shape=jax.ShapeDtypeStruct((n_rows, d), out_dtype),
        compiler_params=_cparams(("arbitrary",), est),
    )(*h_args, gain.reshape(1, d), mods)


def _w_spec(w, lead, kk, tn, row_block=0):
    assert w.ndim == len(lead) + 2
    return pl.BlockSpec((None,) * len(lead) + (kk, tn), lambda j, i: tuple(lead) + (row_block, j))


NORM_ROWS_PER_PASS = 64


def _norm_into(u_scr, h_ref, g_ref, m_ref, gs_scr, k):
    rows, d = u_scr.shape
    ln, sub = V7X_LANES, V7X_SUBLANES
    step = min(NORM_ROWS_PER_PASS, rows)
    gs_scr[0] = jnp.broadcast_to(g_ref[...] * (1.0 + m_ref[0, 3 * k + 1:3 * k + 2, :]), (sub, d))
    gs_scr[1] = jnp.broadcast_to(m_ref[0, 3 * k:3 * k + 1, :], (sub, d))

    def rows_pass(it, carry):
        r0 = pl.multiple_of(it * step, step)
        acc = jnp.zeros((step, ln), F32)
        for c in range(0, d, ln):
            x = h_ref[pl.ds(r0, step), c:c + ln]
            acc = acc + x * x
        ms = jnp.sum(acc, axis=-1, keepdims=True) * (1.0 / d)
        rinv = jnp.broadcast_to(lax.rsqrt(ms + EPS), (step, ln))
        for c in range(0, d, ln):
            x = (h_ref[pl.ds(r0, step), c:c + ln] * rinv).reshape(step // sub, sub, ln)
            y = x * gs_scr[0, :, c:c + ln] + gs_scr[1, :, c:c + ln]
            u_scr[pl.ds(r0, step), c:c + ln] = y.reshape(step, ln).astype(u_scr.dtype)
        return carry

    lax.fori_loop(0, rows // step, rows_pass, 0)


def _mm_proj_body(h_ref, g_ref, m_ref, w_ref, ws_ref, o_ref, os_ref, u_scr, gs_scr, *, k):
    @pl.when(pl.program_id(1) == 0)
    def _():
        _norm_into(u_scr, h_ref, g_ref, m_ref, gs_scr, k)
        os_ref[...] = jnp.dot(u_scr[...], ws_ref[...], preferred_element_type=F32).astype(os_ref.dtype)

    o_ref[...] = jnp.dot(u_scr[...], w_ref[...], preferred_element_type=F32).astype(o_ref.dtype)


def _norm_matmul(h, gain, mods, k, w, w_side, rows, tn_cap):
    r, d = h.shape
    n = w.shape[-1]
    ns = w_side.shape[-1]
    tm = rows.tile(512)
    tn = _pick(n, tn_cap, V7X_LANES)
    est = (2 * tm * d * 4 + tm * d * 2 + 2 * d * (tn + ns) * 2 + 2 * tm * (tn + ns) * 2
           + 2 * tm * tn * 4 + 6 * NORM_ROWS_PER_PASS * d * 4)
    return pl.pallas_call(
        functools.partial(_mm_proj_body, k=k),
        grid=(r // tm, n // tn),
        in_specs=[
            pl.BlockSpec((tm, d), lambda i, j: (i, 0)),
            pl.BlockSpec((1, d), lambda i, j: (0, 0)),
            pl.BlockSpec((1, 9, d), lambda i, j: (rows.set_of_tile(i, tm), 0, 0)),
            pl.BlockSpec((d, tn), lambda i, j: (0, j)),
            pl.BlockSpec((d, ns), lambda i, j: (0, 0)),
        ],
        out_specs=[pl.BlockSpec((tm, tn), lambda i, j: (i, j)),
                   pl.BlockSpec((tm, ns), lambda i, j: (i, 0))],
        out_shape=[jax.ShapeDtypeStruct((r, n), BF16), jax.ShapeDtypeStruct((r, ns), BF16)],
        scratch_shapes=[pltpu.VMEM((tm, d), BF16), pltpu.VMEM((2, V7X_SUBLANES, d), F32)],
        compiler_params=_cparams(("arbitrary", "arbitrary"), est),
    )(h, gain.reshape(1, d), mods, w, w_side)


CAST_ROWS_PER_PASS = 512


def _mm_up_cast_body(x_ref, wg_ref, wu_ref, o_ref, wg_scr, wu_scr):
    @pl.when(pl.program_id(1) == 0)
    def _():
        kk = wg_scr.shape[0]
        step = min(CAST_ROWS_PER_PASS, kk)
        for r0 in range(0, kk, step):
            wg_scr[r0:r0 + step, :] = wg_ref[r0:r0 + step, :].astype(BF16)
            wu_scr[r0:r0 + step, :] = wu_ref[r0:r0 + step, :].astype(BF16)

    x = x_ref[...]
    g = jnp.dot(x, wg_scr[...], preferred_element_type=F32)
    u = jnp.dot(x, wu_scr[...], preferred_element_type=F32)
    o_ref[...] = (g * jax.nn.sigmoid(g) * u).astype(o_ref.dtype)


def _ffn_up(u, w_gate, w_up, lead, rows):
    r, k = u.shape
    n = w_gate.shape[-1]
    tm = rows.tile(512)
    tn = _pick(n, 512, V7X_LANES)
    est = 2 * tm * k * 2 + 4 * k * tn * 4 + 2 * k * tn * 2 + 2 * tm * tn * 2 + 2 * tm * tn * 4
    return pl.pallas_call(
        _mm_up_cast_body,
        grid=(n // tn, r // tm),
        in_specs=[
            pl.BlockSpec((tm, k), lambda j, i: (i, 0)),
            _w_spec(w_gate, lead, k, tn),
            _w_spec(w_up, lead, k, tn),
        ],
        out_specs=pl.BlockSpec((tm, tn), lambda j, i: (i, j)),
        out_shape=jax.ShapeDtypeStruct((r, n), BF16),
        scratch_shapes=[pltpu.VMEM((k, tn), BF16), pltpu.VMEM((k, tn), BF16)],
        compiler_params=_cparams(("arbitrary", "arbitrary"), est),
    )(u, w_gate, w_up)


def _mm_res_body(*refs, n_pairs, n_h, n_lat_tiles, coef, k):
    h_refs = refs[2 * n_pairs:2 * n_pairs + n_h]
    m_ref, o_ref = refs[2 * n_pairs + n_h:]
    acc = None
    for p in range(n_pairs):
        part = jnp.dot(refs[2 * p][...], refs[2 * p + 1][...], preferred_element_type=F32)
        acc = part if acc is None else acc + part
    gate = m_ref[0, 3 * k + 2:3 * k + 3, :]
    o_ref[...] = _read_h(h_refs, pl.program_id(1), n_lat_tiles) + (coef * gate) * acc


def _res_matmul(xs, w, lead, h, mods, k, coef, rows, n_rows):
    n = w.shape[-1]
    tm = rows.tile(512)
    tn = _pick(n, 512, V7X_LANES)
    in_specs, args, est = [], [], 0
    for p, x in enumerate(xs):
        kk = x.shape[1]
        assert kk * len(xs) == w.shape[-2]
        in_specs += [pl.BlockSpec((tm, kk), lambda j, i: (i, 0)), _w_spec(w, lead, kk, tn, p)]
        args += [x, w]
        est += 2 * tm * kk * 2 + 2 * kk * tn * 2
    h_specs, h_args = _h_specs(h, rows, tm, tn, lambda j, i: (i, j))
    in_specs += h_specs + [pl.BlockSpec((1, 9, tn), lambda j, i: (rows.set_of_tile(i, tm), 0, j))]
    est += (4 + 2 * len(h_args)) * tm * tn * 4
    return pl.pallas_call(
        functools.partial(_mm_res_body, n_pairs=len(xs), n_h=len(h_args),
                          n_lat_tiles=rows.n_lat // tm, coef=coef, k=k),
        grid=(n // tn, n_rows // tm),
        in_specs=in_specs,
        out_specs=pl.BlockSpec((tm, tn), lambda j, i: (i, j)),
        out_shape=jax.ShapeDtypeStruct((rows.R, n), F32),
        compiler_params=_cparams(("arbitrary", "arbitrary"), est),
    )(*args, *h_args, mods)


def _rope_pairs(t, c, s1, s2):
    half = MLA_ROPE // 2
    return t * c + pltpu.roll(t, V7X_LANES - half, 1) * s1 + pltpu.roll(t, half, 1) * s2


def _lowrank_body(x_ref, g_ref, w_ref, c_ref, s1_ref, s2_ref, o_ref, *, heads, rope):
    x = x_ref[...].astype(F32)
    ms = jnp.mean(x * x, axis=-1, keepdims=True)
    xn = ((x * lax.rsqrt(ms + EPS)) * g_ref[...]).astype(BF16)
    hw = 2 * V7X_LANES
    for h in range(heads):
        acc = jnp.dot(xn, w_ref[:, h * hw:(h + 1) * hw], preferred_element_type=F32)
        if rope:
            o_ref[:, h * hw:h * hw + V7X_LANES] = acc[:, :V7X_LANES].astype(o_ref.dtype)
            t = _rope_pairs(acc[:, V7X_LANES:], c_ref[...], s1_ref[...], s2_ref[...])
            o_ref[:, h * hw + V7X_LANES:(h + 1) * hw] = t.astype(o_ref.dtype)
        else:
            o_ref[:, h * hw:(h + 1) * hw] = acc.astype(o_ref.dtype)


def _lowrank_up(p, col_block, k, gain, w, tables, rows, heads, rope):
    r = p.shape[0]
    n = w.shape[1]
    tm = rows.tile(512)
    est = 2 * tm * k * 2 + 2 * k * n * 2 + 2 * tm * n * 2 + 6 * tm * V7X_LANES * 4 + tm * k * 8 + 4 * tm * 256 * 4
    tab_spec = pl.BlockSpec((tm, V7X_LANES), lambda i: (i, 0))
    return pl.pallas_call(
        functools.partial(_lowrank_body, heads=heads, rope=rope),
        grid=(r // tm,),
        in_specs=[
            pl.BlockSpec((tm, k), lambda i: (i, col_block)),
            pl.BlockSpec((1, k), lambda i: (0, 0)),
            pl.BlockSpec((k, n), lambda i: (0, 0)),
            tab_spec, tab_spec, tab_spec,
        ],
        out_specs=pl.BlockSpec((tm, n), lambda i: (i, 0)),
        out_shape=jax.ShapeDtypeStruct((r, n), BF16),
        compiler_params=_cparams(("arbitrary",), est),
    )(p, gain.reshape(1, k), w, *tables)


def _kr_rope_body(x_ref, c_ref, s1_ref, s2_ref, o_ref):
    t = _rope_pairs(x_ref[...].astype(F32), c_ref[...], s1_ref[...], s2_ref[...])
    o_ref[...] = t.astype(o_ref.dtype)


def _kr_rope(p, col_block, tables, rows):
    r = p.shape[0]
    tm = rows.tile(512)
    spec = pl.BlockSpec((tm, V7X_LANES), lambda i: (i, 0))
    return pl.pallas_call(
        _kr_rope_body,
        grid=(r // tm,),
        in_specs=[pl.BlockSpec((tm, V7X_LANES), lambda i: (i, col_block)), spec, spec, spec],
        out_specs=spec,
        out_shape=jax.ShapeDtypeStruct((r, V7X_LANES), BF16),
        compiler_params=_cparams(("arbitrary",), 16 * tm * V7X_LANES * 4),
    )(p, *tables)


ATTN_VT_ROWS = V7X_LANES + 16


def _attn_body(*refs, n_ctx, n_lat, tk):
    if n_lat:
        (q_ref, kc_ref, vc_ref, krc_ref, kl_ref, vl_ref, krl_ref,
         o_ref, kcat, vt_c, vt_l, acc_scr, s_a, s_b) = refs
    else:
        q_ref, kc_ref, vc_ref, krc_ref, o_ref, kcat, vt_c, acc_scr, s_a = refs
    ln = V7X_LANES
    nt = (((1,), (1,)), ((), ()))

    @pl.when(pl.program_id(2) == 0)
    def _():
        kcat[0:n_ctx, 0:ln] = kc_ref[...]
        kcat[0:n_ctx, ln:2 * ln] = krc_ref[...]
        eye = (lax.broadcasted_iota(jnp.int32, (ln, ln), 0)
               == lax.broadcasted_iota(jnp.int32, (ln, ln), 1)).astype(BF16)

        def ones_rows(n):
            first = lax.broadcasted_iota(jnp.int32, (ATTN_VT_ROWS - ln, n), 0) == 0
            return jnp.where(first, 1.0, 0.0).astype(BF16)

        vt_c[0:ln, :] = lax.dot_general(eye, vc_ref[...], nt, preferred_element_type=F32).astype(BF16)
        vt_c[ln:, :] = ones_rows(n_ctx)
        if n_lat:
            kcat[n_ctx:n_ctx + n_lat, 0:ln] = kl_ref[...]
            kcat[n_ctx:n_ctx + n_lat, ln:2 * ln] = krl_ref[...]
            for c in range(n_lat // tk):
                vt_l[c, 0:ln, :] = lax.dot_general(eye, vl_ref[c * tk:(c + 1) * tk, :], nt,
                                                   preferred_element_type=F32).astype(BF16)
                vt_l[c, ln:, :] = ones_rows(tk)

    q = q_ref[...]
    tq = q.shape[0]
    acc_scr[...] = jnp.zeros(acc_scr.shape, F32)

    def scores(k, dst):
        dst[0:k.shape[0], :] = lax.dot_general(k, q, nt, preferred_element_type=F32)

    def softmax_pv(src, n, v_t, m_prev):
        s = src[0:n, :]
        m_next = jnp.maximum(m_prev, jnp.max(s, axis=0, keepdims=True))
        p = jnp.exp2(s - m_next).astype(BF16)
        alpha = jnp.exp2(m_prev - m_next)
        acc_scr[...] = alpha * acc_scr[...] + jnp.dot(v_t, p, preferred_element_type=F32)
        return m_next

    def lat_keys(c):
        return kcat[pl.ds(pl.multiple_of(n_ctx + c * tk, ln), tk), :]

    m = jnp.full((1, tq), -jnp.inf, F32)
    scores(kcat[0:n_ctx, :], s_a)
    if not n_lat:
        m = softmax_pv(s_a, n_ctx, vt_c[...], m)
    else:
        n_ch = n_lat // tk
        scores(lat_keys(0), s_b)
        m = softmax_pv(s_a, n_ctx, vt_c[...], m)

        def pair(i, m):
            c = 2 * i
            scores(lat_keys(c + 1), s_a)
            m = softmax_pv(s_b, tk, vt_l[c], m)
            scores(lat_keys(c + 2), s_b)
            return softmax_pv(s_a, tk, vt_l[c + 1], m)

        m = lax.fori_loop(0, n_ch // 2 - 1, pair, m)
        scores(lat_keys(n_ch - 1), s_a)
        m = softmax_pv(s_b, tk, vt_l[n_ch - 2], m)
        m = softmax_pv(s_a, tk, vt_l[n_ch - 1], m)
    out = acc_scr[0:ln, :] / acc_scr[ln:ln + 1, :]
    o_ref[...] = out.T.astype(o_ref.dtype)


def _attention(q, kv, kr, rows, heads, latent):
    b, t, tc = rows.B, rows.T, rows.TC
    ln = V7X_LANES
    ctx_blk0 = rows.n_lat // tc
    if latent:
        tq = _pick(t, 1024, ln)
        nq = t // tq
        tk = _pick(t // 2, 1024, ln)
        assert tk >= tc
        n_lat = t
    else:
        tq, nq, tk, n_lat = tc, 1, tc, 0
    q_row = (lambda bb, qi: bb * nq + qi) if latent else (lambda bb, qi: ctx_blk0 * (tc // tq) + bb)
    in_specs = [
        pl.BlockSpec((tq, 2 * ln), lambda bb, h, qi: (q_row(bb, qi), h)),
        pl.BlockSpec((tc, ln), lambda bb, h, qi: (ctx_blk0 + bb, 2 * h)),
        pl.BlockSpec((tc, ln), lambda bb, h, qi: (ctx_blk0 + bb, 2 * h + 1)),
        pl.BlockSpec((tc, ln), lambda bb, h, qi: (ctx_blk0 + bb, 0)),
    ]
    args = [q, kv, kv, kr]
    scratch = [pltpu.VMEM((tc + n_lat, 2 * ln), BF16), pltpu.VMEM((ATTN_VT_ROWS, tc), BF16)]
    if latent:
        in_specs += [
            pl.BlockSpec((t, ln), lambda bb, h, qi: (bb, 2 * h)),
            pl.BlockSpec((t, ln), lambda bb, h, qi: (bb, 2 * h + 1)),
            pl.BlockSpec((t, ln), lambda bb, h, qi: (bb, 0)),
        ]
        args += [kv, kv, kr]
        scratch.append(pltpu.VMEM((n_lat // tk, ATTN_VT_ROWS, tk), BF16))
    scratch.append(pltpu.VMEM((ATTN_VT_ROWS, tq), F32))
    scratch += [pltpu.VMEM((tk, tq), F32)] * (2 if latent else 1)
    nk = tc + n_lat
    est = (2 * tq * 2 * ln * 2 + 6 * tc * ln * 2 + 6 * n_lat * ln * 2 + nk * 3 * ln * 2
           + 2 * tq * ln * 2 + 2 * tq * ln * 4 + 5 * tq * tk * 4)
    return pl.pallas_call(
        functools.partial(_attn_body, n_ctx=tc, n_lat=n_lat, tk=tk),
        grid=(b, heads, nq),
        in_specs=in_specs,
        out_specs=pl.BlockSpec((tq, ln), lambda bb, h, qi: (q_row(bb, qi), h)),
        out_shape=jax.ShapeDtypeStruct((rows.R, heads * ln), BF16),
        scratch_shapes=scratch,
        compiler_params=_cparams(("arbitrary", "arbitrary", "arbitrary"), est),
    )(*args)


def _scan_rows(a, bv, reverse):
    n = a.shape[0]
    s = 1
    while s < n:
        a_sh = _shift_rows(a, s, 1.0, reverse)
        b_sh = _shift_rows(bv, s, 0.0, reverse)
        bv = a * b_sh + bv
        a = a * a_sh
        s *= 2
    return a, bv


def _shift_rows(x, s, fill, reverse):
    n = x.shape[0]
    if s % V7X_SUBLANES == 0:
        pad = jnp.full((s,) + x.shape[1:], fill, x.dtype)
        return jnp.concatenate([x[s:], pad], axis=0) if reverse else jnp.concatenate([pad, x[:n - s]], axis=0)
    row = lax.broadcasted_iota(jnp.int32, x.shape, 0)
    if reverse:
        return jnp.where(row < (n - s), pltpu.roll(x, n - s, 0), fill)
    return jnp.where(row >= s, pltpu.roll(x, s, 0), fill)


def _cumsum_rows(x, reverse):
    n = x.shape[0]
    s = 1
    while s < n:
        x = x + _shift_rows(x, s, 0.0, reverse)
        s *= 2
    return x


def _gelu_tanh(x):
    return 0.5 * x * (1.0 + jnp.tanh(math.sqrt(2.0 / math.pi) * (x + 0.044715 * (x * x * x))))


def _lru_body(xl_ref, xc_ref, gl_ref, gc_ref, cw_ref, cb_ref, wa_ref, ba_ref, wi_ref, bi_ref, lam_ref,
              yl_ref, yc_ref, cv_l, cv_c, hf_l, hf_c, *, seq, ctx, chunk, conv_chunk):
    pad = V7X_SUBLANES
    zeros8 = jnp.zeros((pad, V7X_LANES), F32)

    def conv_into(src_ref, dst, n, step):
        for c0 in range(0, n, step):
            lo = zeros8 if c0 == 0 else src_ref[c0 - pad:c0, :].astype(F32)
            hi = zeros8 if c0 + step >= n else src_ref[c0 + step:c0 + step + pad, :].astype(F32)
            ext = jnp.concatenate([lo, src_ref[c0:c0 + step, :].astype(F32), hi], axis=0)
            out = cb_ref[...]
            for kk in range(CONV_W):
                off = pad + kk - CONV_PAD_L
                out = out + ext[off:off + step, :] * cw_ref[kk:kk + 1, :]
            dst[c0:c0 + step, :] = out

    conv_into(xl_ref, cv_l, seq, conv_chunk)
    conv_into(xc_ref, cv_c, ctx, min(ctx, conv_chunk))

    for d, reverse in enumerate((False, True)):
        lam = lam_ref[d:d + 1, :]
        neg_sp = -LRU_C * (jnp.maximum(-lam, 0.0) + jnp.log(1.0 + jnp.exp(-jnp.abs(lam))))
        wa = wa_ref[d, 0]
        wi = wi_ref[d, 0]
        ba = ba_ref[d:d + 1, :]
        bi = bi_ref[d:d + 1, :]

        def block(x, carry):
            xg = x.astype(BF16)
            r = jax.nn.sigmoid(jnp.dot(xg, wa, preferred_element_type=F32) + ba)
            i = jax.nn.sigmoid(jnp.dot(xg, wi, preferred_element_type=F32) + bi)
            a = jnp.exp(neg_sp * r)
            bv = jnp.sqrt(1.0 - a * a) * (i * x)
            a_cum, h = _scan_rows(a, bv, reverse)
            h = a_cum * carry + h
            n = x.shape[0]
            new_carry = h[0:1, :] if reverse else h[n - 1:n, :]
            return h, new_carry

        def emit(dst_ref, fwd_scr, g_ref, start, n, h):
            if not reverse:
                fwd_scr[pl.ds(start, n), :] = h
            else:
                tot = fwd_scr[pl.ds(start, n), :] + h
                g = g_ref[pl.ds(start, n), :].astype(F32)
                dst_ref[pl.ds(start, n), :] = (tot * _gelu_tanh(g)).astype(dst_ref.dtype)

        carry = jnp.zeros((1, V7X_LANES), F32)
        cchunk = min(ctx, chunk)
        n_cc = ctx // cchunk
        order = range(n_cc - 1, -1, -1) if reverse else range(n_cc)
        lat_init = None
        for c in order:
            h, carry = block(cv_c[c * cchunk:(c + 1) * cchunk, :], carry)
            if lat_init is None:
                lat_init = h[cchunk - 1:cchunk, :] if reverse else h[0:1, :]
            emit(yc_ref, hf_c, gc_ref, c * cchunk, cchunk, h)
        carry = lat_init

        n_lc = seq // chunk

        def body(it, carry):
            c = (n_lc - 1 - it) if reverse else it
            start = pl.multiple_of(c * chunk, chunk)
            h, carry = block(cv_l[pl.ds(start, chunk), :], carry)
            emit(yl_ref, hf_l, gl_ref, start, chunk, h)
            return carry

        lax.fori_loop(0, n_lc, body, carry)


def _rglru(p, xb_col, gb_col, width, conv_w, conv_b, w_a, b_a, w_i, b_i, lam, rows):
    b, t, tc = rows.B, rows.T, rows.TC
    ln = V7X_LANES
    groups = width // ln
    ctx_blk0 = rows.n_lat // tc
    chunk = _pick(t, 256, V7X_SUBLANES)
    conv_chunk = _pick(t, 1024, V7X_SUBLANES)
    vec = lambda rws: pl.BlockSpec((rws, ln), lambda bb, g: (0, g))
    est = 2 * (t + tc) * ln * (2 + 2 + 2) + 2 * (t + tc) * ln * 4 + 64 * chunk * ln * 4
    return pl.pallas_call(
        functools.partial(_lru_body, seq=t, ctx=tc, chunk=chunk, conv_chunk=conv_chunk),
        grid=(b, groups),
        in_specs=[
            pl.BlockSpec((t, ln), lambda bb, g: (bb, xb_col + g)),
            pl.BlockSpec((tc, ln), lambda bb, g: (ctx_blk0 + bb, xb_col + g)),
            pl.BlockSpec((t, ln), lambda bb, g: (bb, gb_col + g)),
            pl.BlockSpec((tc, ln), lambda bb, g: (ctx_blk0 + bb, gb_col + g)),
            vec(CONV_W), vec(1),
            pl.BlockSpec((2, 1, ln, ln), lambda bb, g: (0, g, 0, 0)), vec(2),
            pl.BlockSpec((2, 1, ln, ln), lambda bb, g: (0, g, 0, 0)), vec(2),
            vec(2),
        ],
        out_specs=[
            pl.BlockSpec((t, ln), lambda bb, g: (bb, g)),
            pl.BlockSpec((tc, ln), lambda bb, g: (bb, g)),
        ],
        out_shape=[jax.ShapeDtypeStruct((rows.R, width), BF16),
                   jax.ShapeDtypeStruct((b * tc, width), BF16)],
        scratch_shapes=[pltpu.VMEM((t, ln), F32), pltpu.VMEM((tc, ln), F32),
                        pltpu.VMEM((t, ln), F32), pltpu.VMEM((tc, ln), F32)],
        compiler_params=_cparams(("arbitrary", "arbitrary"), est),
    )(p, p, p, p, conv_w, conv_b.reshape(1, width), w_a, b_a, w_i, b_i, lam)


def _log_sigmoid(x):
    return jnp.minimum(x, 0.0) - jnp.log(1.0 + jnp.exp(-jnp.abs(x)))


def _linattn_body(*refs, kind, reverse, tb, dk, dv):
    it = iter(refs)
    q_ref, k_ref, v_ref = next(it), next(it), next(it)
    if kind == "ret":
        cos_ref, sin_ref, dec_ref = next(it), next(it), next(it)
    else:
        ga_ref, wg_ref, bg_ref = next(it), next(it), next(it)
    if reverse:
        of_ref, g_ref, gain_ref = next(it), next(it), next(it)
    o_ref, st = next(it), next(it)
    hp = st.shape[0]

    @pl.when(pl.program_id(2) == 0)
    def _():
        st[...] = jnp.zeros(st.shape, F32)

    n_ch = tb // CHUNK
    ri = lax.broadcasted_iota(jnp.int32, (CHUNK, CHUNK), 0)
    ci = lax.broadcasted_iota(jnp.int32, (CHUNK, CHUNK), 1)
    mask = (ci >= ri) if reverse else (ci <= ri)
    nt = (((1,), (1,)), ((), ()))
    tn = (((0,), (0,)), ((), ()))
    pos = lax.broadcasted_iota(jnp.int32, (tb, 1), 0) % CHUNK
    n_terms = ((CHUNK - pos) if reverse else (pos + 1)).astype(F32)
    for hh in range(hp):
        q = q_ref[:, hh * dk:(hh + 1) * dk].astype(F32) * (1.0 if kind == "ret" else dk ** -0.5)
        k = k_ref[:, hh * dk:(hh + 1) * dk].astype(F32) * (dk ** -0.5 if kind == "ret" else 1.0)
        if kind == "ret":
            c, s = cos_ref[...], sin_ref[...]
            hd = dk // 2
            q = jnp.concatenate([q[:, :hd] * c - q[:, hd:] * s, q[:, :hd] * s + q[:, hd:] * c], axis=1)
            k = jnp.concatenate([k[:, :hd] * c - k[:, hd:] * s, k[:, :hd] * s + k[:, hd:] * c], axis=1)
            bcum_all = n_terms * dec_ref[hh]
        else:
            z = jnp.dot(ga_ref[...], wg_ref[0, :, hh * dk:(hh + 1) * dk], preferred_element_type=F32)
            la = _log_sigmoid(z + bg_ref[0, :, hh * dk:(hh + 1) * dk]) * (1.0 / GLA_TAU)
        outs = [None] * n_ch
        for c in (range(n_ch - 1, -1, -1) if reverse else range(n_ch)):
            sl = slice(c * CHUNK, (c + 1) * CHUNK)
            bcum = bcum_all[sl] if kind == "ret" else _cumsum_rows(la[sl], reverse)
            bend = bcum[0:1, :] if reverse else bcum[CHUNK - 1:CHUNK, :]
            qe = (q[sl] * jnp.exp(bcum)).astype(BF16)
            ke = (k[sl] * jnp.exp(-bcum)).astype(BF16)
            ks = (k[sl] * jnp.exp(bend - bcum)).astype(BF16)
            vc = v_ref[sl, hh * dv:(hh + 1) * dv]
            att = lax.dot_general(qe, ke, nt, preferred_element_type=F32)
            att = jnp.where(mask, att, 0.0).astype(BF16)
            o = jnp.dot(att, vc, preferred_element_type=F32)
            o = o + lax.dot_general(qe, st[hh].astype(BF16), nt, preferred_element_type=F32)
            st[hh] = st[hh] * jnp.exp(bend) + lax.dot_general(vc, ks, tn, preferred_element_type=F32)
            outs[c] = o
        o = jnp.concatenate(outs, axis=0)
        cs = slice(hh * dv, (hh + 1) * dv)
        if not reverse:
            o_ref[:, cs] = o
        else:
            y = of_ref[:, cs] + o
            ms = jnp.mean(y * y, axis=-1, keepdims=True)
            y = (y * lax.rsqrt(ms + EPS)) * gain_ref[:, cs]
            g = g_ref[:, cs].astype(F32)
            o_ref[:, cs] = (y * (g * jax.nn.sigmoid(g))).astype(o_ref.dtype)


def _linattn(p, cols, heads, kind, reverse, rows, extra, o_fwd=None, gain=None):
    b, t, tc = rows.B, rows.T, rows.TC
    dk, dv = RET_DK, RET_DV
    tb = tc
    assert t % tb == 0 and tb % CHUNK == 0
    nt = 1 + t // tb
    ctx_blk0 = rows.n_lat // tb
    per_seq = t // tb

    def row_blk(bb, tt):
        lat = (per_seq - tt) if reverse else (tt - 1)
        return jnp.where(tt == 0, ctx_blk0 + bb, bb * per_seq + lat)

    qc, kc, vc, gc = cols
    hp = math.gcd(heads, LINATTN_HEADS_PER_STEP)
    wk, wv = hp * dk, hp * dv
    assert all(o % wk == 0 for o in (qc, kc)) and all(o % wv == 0 for o in (vc, gc))
    in_specs = [
        pl.BlockSpec((tb, wk), lambda bb, h, tt: (row_blk(bb, tt), qc // wk + h)),
        pl.BlockSpec((tb, wk), lambda bb, h, tt: (row_blk(bb, tt), kc // wk + h)),
        pl.BlockSpec((tb, wv), lambda bb, h, tt: (row_blk(bb, tt), vc // wv + h)),
    ]
    args = [p, p, p]
    if kind == "ret":
        cos, sin, dec = extra
        tab = pl.BlockSpec((tb, dk // 2), lambda bb, h, tt: (row_blk(bb, tt), 0))
        in_specs += [tab, tab, pl.BlockSpec((hp, 1, dk), lambda bb, h, tt: (h, 0, 0))]
        args += [cos, sin, dec]
    else:
        ga_rows, wg, bg = extra
        in_specs += [
            pl.BlockSpec((tb, V7X_LANES), lambda bb, h, tt: (row_blk(bb, tt), 0)),
            pl.BlockSpec((1, V7X_LANES, wk), lambda bb, h, tt: (0, 0, h)),
            pl.BlockSpec((1, 1, wk), lambda bb, h, tt: (0, 0, h)),
        ]
        args += [ga_rows, wg, bg]
    if reverse:
        in_specs += [
            pl.BlockSpec((tb, wv), lambda bb, h, tt: (row_blk(bb, tt), h)),
            pl.BlockSpec((tb, wv), lambda bb, h, tt: (row_blk(bb, tt), gc // wv + h)),
            pl.BlockSpec((1, wv), lambda bb, h, tt: (0, h)),
        ]
        args += [o_fwd, p, gain]
    est = hp * (2 * tb * (2 * dk + dv) * 2 + 6 * tb * dv * 4 + 3 * dv * dk * 4 + 24 * tb * dk * 4 + 4 * tb * dv * 4)
    return pl.pallas_call(
        functools.partial(_linattn_body, kind=kind, reverse=reverse, tb=tb, dk=dk, dv=dv),
        grid=(b, heads // hp, nt),
        in_specs=in_specs,
        out_specs=pl.BlockSpec((tb, wv), lambda bb, h, tt: (row_blk(bb, tt), h)),
        out_shape=jax.ShapeDtypeStruct((rows.R, heads * dv), BF16 if reverse else F32),
        scratch_shapes=[pltpu.VMEM((hp, dv, dk), F32)],
        compiler_params=_cparams(("arbitrary", "arbitrary", "arbitrary"), est),
    )(*args)


def _pad_cols(w, n):
    return jnp.pad(w, ((0, 0), (0, n - w.shape[1])))


def _axial_angles(t, rot_dim):
    n_rows = t // GRID_W
    row = jnp.repeat(jnp.arange(n_rows, dtype=F32), GRID_W)
    col = jnp.tile(jnp.arange(GRID_W, dtype=F32), n_rows)
    n_freq = rot_dim // 4
    inv = ROPE_BASE ** (-jnp.arange(n_freq, dtype=F32) / n_freq)
    return jnp.concatenate([row[:, None] * inv, col[:, None] * inv], axis=-1)


def _row_tables(rows, rot_dim):
    ang = _axial_angles(rows.T, rot_dim)
    half = rot_dim // 2
    cos = jnp.concatenate([jnp.tile(jnp.cos(ang), (rows.B, 1)), jnp.ones((rows.n_ctx, half), F32)], axis=0)
    sin = jnp.concatenate([jnp.tile(jnp.sin(ang), (rows.B, 1)), jnp.zeros((rows.n_ctx, half), F32)], axis=0)
    return cos, sin


def _mla_tables(rows):
    cos, sin = _row_tables(rows, MLA_ROPE)
    half = MLA_ROPE // 2
    z = jnp.zeros((rows.R, V7X_LANES - MLA_ROPE), F32)
    zh = jnp.zeros((rows.R, half), F32)
    c = jnp.concatenate([cos, cos, z], axis=1)
    s1 = jnp.concatenate([-sin, zh, z], axis=1)
    s2 = jnp.concatenate([zh, sin, z], axis=1)
    return c, s1, s2


def _even_mixer(h, gain, mods, rows, n_out, w_in, q_norm, w_uq, kv_norm, w_ukv, conv_w, conv_b,
                w_a, b_a, w_i, b_i, lam, w_out, tables):
    q_rank, kv_rank, lru_w = q_norm.shape[0], kv_norm.shape[0], conv_w.shape[1]
    ln = V7X_LANES
    o_cq, o_ckv, o_kr, o_xb, o_gb = np.cumsum([0, q_rank, kv_rank, MLA_ROPE, lru_w]).tolist()
    w_main = jnp.concatenate([w_in[:, :o_kr], w_in[:, o_xb:]], axis=1).astype(BF16)
    w_kr = _pad_cols(w_in[:, o_kr:o_xb], ln).astype(BF16)
    p, kr_raw = _norm_matmul(h, gain, mods, 1, w_main, w_kr, rows, 4 * V7X_MXU_DIM)
    c_cq, c_ckv, c_xb, c_gb = 0, q_rank, q_rank + kv_rank, q_rank + kv_rank + lru_w

    scale = (MLA_NOPE + MLA_ROPE) ** -0.5 * math.log2(math.e)
    wq = (w_uq * scale).reshape(q_rank, MLA_HEADS, MLA_NOPE + MLA_ROPE)
    wq = jnp.pad(wq, ((0, 0), (0, 0), (0, 2 * ln - MLA_NOPE - MLA_ROPE))).reshape(q_rank, MLA_HEADS * 2 * ln)
    assert c_cq % q_rank == 0 and c_ckv % kv_rank == 0
    q = _lowrank_up(p, c_cq // q_rank, q_rank, q_norm, wq.astype(BF16), tables, rows, MLA_HEADS, True)
    kv = _lowrank_up(p, c_ckv // kv_rank, kv_rank, kv_norm, w_ukv.astype(BF16), tables, rows, MLA_HEADS, False)
    kr = _kr_rope(kr_raw, 0, tables, rows)

    att = _attention(q, kv, kr, rows, MLA_HEADS, latent=True)
    att_c = _attention(q, kv, kr, rows, MLA_HEADS, latent=False)
    att = lax.dynamic_update_slice(att, att_c[rows.n_lat:], (rows.n_lat, 0))

    y_l, y_c = _rglru(p, c_xb // ln, c_gb // ln, lru_w, conv_w, conv_b,
                      w_a.astype(BF16), b_a, w_i.astype(BF16), b_i, lam, rows)
    lru = lax.dynamic_update_slice(y_l, y_c, (rows.n_lat, 0))
    n_att = MLA_HEADS * MLA_V
    assert MLA_HEADS * MLA_V == lru_w
    return _res_matmul([att, lru], w_out.astype(BF16), (), h, mods, 1, 1.0, rows, n_out)


def _odd_mixer(h, gain, mods, rows, n_out, w_in, ret_log_decay, ret_norm, gla_w_gate2, gla_b_gate, gla_norm,
               w_out, tables):
    ln = V7X_LANES
    sizes = [RET_HEADS * RET_DK, RET_HEADS * RET_DK, RET_HEADS * RET_DV, RET_HEADS * RET_DV,
             GLA_HEADS * GLA_DK, GLA_HEADS * GLA_DK, GLA_HEADS * GLA_DV, GLA_HEADS * GLA_DV]
    offs = np.cumsum([0] + sizes).tolist()
    rq, rk, rv, rg, gq, gk, gv, gr, ga = offs
    w_main = w_in[:, :ga].astype(BF16)
    w_ga = _pad_cols(w_in[:, ga:], ln).astype(BF16)
    p, ga_rows = _norm_matmul(h, gain, mods, 1, w_main, w_ga, rows, 4 * V7X_MXU_DIM)

    cos, sin = tables
    ys = []
    for kind, heads, cols, norm in (("ret", RET_HEADS, (rq, rk, rv, rg), ret_norm),
                                    ("gla", GLA_HEADS, (gq, gk, gv, gr), gla_norm)):
        o_f = None
        for d, reverse in enumerate((False, True)):
            if kind == "ret":
                dec = jnp.broadcast_to(ret_log_decay[d][:, None, None], (heads, 1, RET_DK)).astype(F32)
                extra = (cos, sin, dec)
            else:
                wg = jnp.zeros((1, ln, heads * GLA_DK), F32).at[0, d * GLA_RANK:(d + 1) * GLA_RANK].set(gla_w_gate2[d])
                extra = (ga_rows, wg.astype(BF16), gla_b_gate[d].reshape(1, 1, heads * GLA_DK))
            out = _linattn(p, cols, heads, kind, reverse, rows, extra, o_fwd=o_f,
                           gain=norm.reshape(1, heads * RET_DV))
            if reverse:
                ys.append(out)
            else:
                o_f = out
    n_ret = RET_HEADS * RET_DV
    assert RET_HEADS * RET_DV == GLA_HEADS * GLA_DV
    return _res_matmul(ys, w_out.astype(BF16), (), h, mods, 1, 1.0, rows, n_out)


def kernel(x, c, ctx, c_ctx, ada_w, ada_b, norm_w, ffn_w_gate, ffn_w_up, ffn_w_down, ev_w_in, mla_q_norm, mla_w_uq, mla_kv_norm, mla_w_ukv, lru_conv_w, lru_conv_b, lru_w_a, lru_b_a, lru_w_i, lru_b_i, lru_lambda, ev_w_out, od_w_in, ret_log_decay, ret_norm, gla_w_gate2, gla_b_gate, gla_norm, od_w_out, final_norm_w):
    batch, seq, d = x.shape
    tc = ctx.shape[1]
    depth = ada_w.shape[0]
    rows = _Rows(batch, seq, tc)
    h = (x.reshape(batch * seq, d), ctx.reshape(batch * tc, d))

    n_sets = 1 + batch
    c_rows = jnp.concatenate([c_ctx[None, :], c], axis=0)
    c_rows = jnp.pad(c_rows, ((0, -n_sets % V7X_SUBLANES), (0, 0)))
    mods_all = _ada(c_rows, ada_w, ada_b)[:, :n_sets].reshape(depth, n_sets, 9, d)

    mla_tabs = _mla_tables(rows)
    ret_tabs = _row_tables(rows, RET_DK)
    w_down_bf = ffn_w_down.astype(BF16)

    for l in range(depth):
        mods = mods_all[l]
        n_out = rows.n_lat if l == depth - 1 else rows.R

        def ffn(hh, k, idx, n_rows):
            u = _norm_mod(hh, norm_w[l, k], mods, k, rows, BF16, n_rows)
            a = _ffn_up(u, ffn_w_gate, ffn_w_up, (l, idx), rows)
            return _res_matmul([a], w_down_bf, (l, idx), hh, mods, k, FFN_RES, rows, n_rows)

        h = ffn(h, 0, 0, rows.R)
        if l % 2 == 0:
            e = l // 2
            h = _even_mixer(h, norm_w[l, 1], mods, rows, n_out, ev_w_in[e], mla_q_norm[e], mla_w_uq[e], mla_kv_norm[e],
                            mla_w_ukv[e], lru_conv_w[e], lru_conv_b[e], lru_w_a[e], lru_b_a[e],
                            lru_w_i[e], lru_b_i[e], lru_lambda[e], ev_w_out[e], mla_tabs)
        else:
            o = l // 2
            h = _odd_mixer(h, norm_w[l, 1], mods, rows, n_out, od_w_in[o], ret_log_decay[o], ret_norm[o], gla_w_gate2[o],
                           gla_b_gate[o], gla_norm[o], od_w_out[o], ret_tabs)
        h = ffn(h, 2, 1, n_out)

    out = _norm_mod(h, final_norm_w, mods_all[0], None, rows, F32, rows.n_lat)
    return out.reshape(batch, seq, d)
```

```python
import functools
import math

import jax
import jax.numpy as jnp
import numpy as np
from jax import lax
from jax.experimental import pallas as pl
from jax.experimental.pallas import tpu as pltpu

F32 = jnp.float32
BF16 = jnp.bfloat16

V7X_LANES = 128
V7X_SUBLANES = 8
V7X_MXU_DIM = 256
V7X_VMEM_BYTES = 64 * 1024 * 1024
V7X_VMEM_CEILING = 58 * 1024 * 1024

GRID_W = 64
EPS = 1e-6
ROPE_BASE = 10000.0
FFN_RES = 0.5
MLA_HEADS = 16
MLA_NOPE = 128
MLA_ROPE = 64
MLA_V = 128
LRU_BLOCKS = 16
LRU_C = 8.0
CONV_W = 4
CONV_PAD_L = 2
RET_HEADS = 4
RET_DK = 256
RET_DV = 512
GLA_HEADS = 4
GLA_DK = 256
GLA_DV = 512
GLA_RANK = 16
GLA_TAU = 16.0
CHUNK = 64
LINATTN_HEADS_PER_STEP = 4


def _cparams(sem, vmem_est):
    limit = int(min(V7X_VMEM_CEILING, max(vmem_est * 5 // 4 + (4 << 20), 16 << 20)))
    return pltpu.CompilerParams(dimension_semantics=sem, vmem_limit_bytes=limit)


def _pick(n, cap, mult):
    best = None
    for t in range(mult, min(n, cap) + 1, mult):
        if n % t == 0:
            best = t
    assert best is not None, (n, cap, mult)
    return best


class _Rows:
    def __init__(self, batch, seq, ctx):
        self.B, self.T, self.TC = batch, seq, ctx
        self.n_lat = batch * seq
        self.n_ctx = batch * ctx
        self.R = self.n_lat + self.n_ctx

    def tile(self, cap):
        g = math.gcd(self.T, self.n_ctx)
        return _pick(g, cap, V7X_SUBLANES)

    def set_of_tile(self, i, tm):
        n_lat_tiles = self.n_lat // tm
        per_seq = self.T // tm
        return jnp.where(i < n_lat_tiles, 1 + i // per_seq, 0)


def _ada_body(c_ref, w_ref, b_ref, o_ref):
    s = c_ref[...]
    s = (s * jax.nn.sigmoid(s)).astype(BF16)
    w = w_ref[0].astype(BF16)
    o_ref[0] = jnp.dot(s, w, preferred_element_type=F32) + b_ref[0]


def _ada(c_rows, ada_w, ada_b):
    depth, d, n = ada_w.shape
    rows = c_rows.shape[0]
    tn = _pick(n, 512, V7X_LANES)
    est = 2 * d * tn * 4 + d * tn * 2 + 4 * rows * d * 4
    return pl.pallas_call(
        _ada_body,
        grid=(depth, n // tn),
        in_specs=[
            pl.BlockSpec((rows, d), lambda l, j: (0, 0)),
            pl.BlockSpec((1, d, tn), lambda l, j: (l, 0, j)),
            pl.BlockSpec((1, 1, tn), lambda l, j: (l, 0, j)),
        ],
        out_specs=pl.BlockSpec((1, rows, tn), lambda l, j: (l, 0, j)),
        out_shape=jax.ShapeDtypeStruct((depth, rows, n), F32),
        compiler_params=_cparams(("arbitrary", "arbitrary"), est),
    )(c_rows, ada_w, ada_b.reshape(depth, 1, n))


def _h_specs(h, rows, tm, tn, tile_of):
    if not isinstance(h, tuple):
        return [pl.BlockSpec((tm, tn), tile_of)], [h]
    n_lat_tiles = rows.n_lat // tm

    def lat(*g):
        i, j = tile_of(*g)
        return jnp.minimum(i, n_lat_tiles - 1), j

    def ctx(*g):
        i, j = tile_of(*g)
        return jnp.maximum(i - n_lat_tiles, 0), j

    return [pl.BlockSpec((tm, tn), lat), pl.BlockSpec((tm, tn), ctx)], list(h)


def _read_h(h_refs, row_tile, n_lat_tiles):
    if len(h_refs) == 1:
        return h_refs[0][...]
    return jnp.where(row_tile < n_lat_tiles, h_refs[0][...], h_refs[1][...])


def _norm_body(*refs, k, n_h, n_lat_tiles):
    g_ref, m_ref, o_ref = refs[n_h:]
    x = _read_h(refs[:n_h], pl.program_id(0), n_lat_tiles)
    ms = jnp.mean(x * x, axis=-1, keepdims=True)
    y = (x * lax.rsqrt(ms + EPS)) * g_ref[...]
    if k is not None:
        y = y * (1.0 + m_ref[0, 3 * k + 1:3 * k + 2, :]) + m_ref[0, 3 * k:3 * k + 1, :]
    o_ref[...] = y.astype(o_ref.dtype)


def _norm_mod(h, gain, mods, k, rows, out_dtype, n_rows):
    d = gain.shape[0]
    tm = rows.tile(256 if isinstance(h, tuple) else 512)
    h_specs, h_args = _h_specs(h, rows, tm, d, lambda i: (i, 0))
    est = 2 * tm * d * (4 * len(h_args) + 4) + 2 * tm * d * 4
    return pl.pallas_call(
        functools.partial(_norm_body, k=k, n_h=len(h_args), n_lat_tiles=rows.n_lat // tm),
        grid=(n_rows // tm,),
        in_specs=h_specs + [
            pl.BlockSpec((1, d), lambda i: (0, 0)),
            pl.BlockSpec((1, 9, d), lambda i: (rows.set_of_tile(i, tm), 0, 0)),
        ],
        out_specs=pl.BlockSpec((tm, d), lambda i: (i, 0)),
        out_shape=jax.ShapeDtypeStruct((n_rows, d), out_dtype),
        compiler_params=_cparams(("arbitrary",), est),
    )(*h_args, gain.reshape(1, d), mods)


def _w_spec(w, lead, kk, tn, row_block=0):
    assert w.ndim == len(lead) + 2
    return pl.BlockSpec((None,) * len(lead) + (kk, tn), lambda j, i: tuple(lead) + (row_block, j))


NORM_ROWS_PER_PASS = 64


def _norm_into(u_scr, h_ref, g_ref, m_ref, gs_scr, k):
    rows, d = u_scr.shape
    ln, sub = V7X_LANES, V7X_SUBLANES
    step = min(NORM_ROWS_PER_PASS, rows)
    gs_scr[0] = jnp.broadcast_to(g_ref[...] * (1.0 + m_ref[0, 3 * k + 1:3 * k + 2, :]), (sub, d))
    gs_scr[1] = jnp.broadcast_to(m_ref[0, 3 * k:3 * k + 1, :], (sub, d))

    def rows_pass(it, carry):
        r0 = pl.multiple_of(it * step, step)
        acc = jnp.zeros((step, ln), F32)
        for c in range(0, d, ln):
            x = h_ref[pl.ds(r0, step), c:c + ln]
            acc = acc + x * x
        ms = jnp.sum(acc, axis=-1, keepdims=True) * (1.0 / d)
        rinv = jnp.broadcast_to(lax.rsqrt(ms + EPS), (step, ln))
        for c in range(0, d, ln):
            x = (h_ref[pl.ds(r0, step), c:c + ln] * rinv).reshape(step // sub, sub, ln)
            y = x * gs_scr[0, :, c:c + ln] + gs_scr[1, :, c:c + ln]
            u_scr[pl.ds(r0, step), c:c + ln] = y.reshape(step, ln).astype(u_scr.dtype)
        return carry

    lax.fori_loop(0, rows // step, rows_pass, 0)


def _mm_proj_body(h_ref, g_ref, m_ref, w_ref, ws_ref, o_ref, os_ref, u_scr, gs_scr, *, k):
    @pl.when(pl.program_id(1) == 0)
    def _():
        _norm_into(u_scr, h_ref, g_ref, m_ref, gs_scr, k)
        os_ref[...] = jnp.dot(u_scr[...], ws_ref[...], preferred_element_type=F32).astype(os_ref.dtype)

    o_ref[...] = jnp.dot(u_scr[...], w_ref[...], preferred_element_type=F32).astype(o_ref.dtype)


def _norm_matmul(h, gain, mods, k, w, n, w_side, rows, tn_cap):
    r, d = h.shape
    assert n <= w.shape[-1]
    ns = w_side.shape[-1]
    tm = rows.tile(512)
    tn = _pick(n, tn_cap, V7X_LANES)
    est = (2 * tm * d * 4 + tm * d * 2 + 2 * d * (tn + ns) * 2 + 2 * tm * (tn + ns) * 2
           + 2 * tm * tn * 4 + 6 * NORM_ROWS_PER_PASS * d * 4)
    return pl.pallas_call(
        functools.partial(_mm_proj_body, k=k),
        grid=(r // tm, n // tn),
        in_specs=[
            pl.BlockSpec((tm, d), lambda i, j: (i, 0)),
            pl.BlockSpec((1, d), lambda i, j: (0, 0)),
            pl.BlockSpec((1, 9, d), lambda i, j: (rows.set_of_tile(i, tm), 0, 0)),
            pl.BlockSpec((d, tn), lambda i, j: (0, j)),
            pl.BlockSpec((d, ns), lambda i, j: (0, 0)),
        ],
        out_specs=[pl.BlockSpec((tm, tn), lambda i, j: (i, j)),
                   pl.BlockSpec((tm, ns), lambda i, j: (i, 0))],
        out_shape=[jax.ShapeDtypeStruct((r, n), BF16), jax.ShapeDtypeStruct((r, ns), BF16)],
        scratch_shapes=[pltpu.VMEM((tm, d), BF16), pltpu.VMEM((2, V7X_SUBLANES, d), F32)],
        compiler_params=_cparams(("arbitrary", "arbitrary"), est),
    )(h, gain.reshape(1, d), mods, w, w_side)


CAST_ROWS_PER_PASS = 512


def _mm_up_cast_body(x_ref, wg_ref, wu_ref, o_ref, wg_scr, wu_scr):
    @pl.when(pl.program_id(1) == 0)
    def _():
        kk = wg_scr.shape[0]
        step = min(CAST_ROWS_PER_PASS, kk)
        for r0 in range(0, kk, step):
            wg_scr[r0:r0 + step, :] = wg_ref[r0:r0 + step, :].astype(BF16)
            wu_scr[r0:r0 + step, :] = wu_ref[r0:r0 + step, :].astype(BF16)

    x = x_ref[...]
    g = jnp.dot(x, wg_scr[...], preferred_element_type=F32)
    u = jnp.dot(x, wu_scr[...], preferred_element_type=F32)
    o_ref[...] = (g * jax.nn.sigmoid(g) * u).astype(o_ref.dtype)


def _ffn_up(u, w_gate, w_up, lead, rows):
    r, k = u.shape
    n = w_gate.shape[-1]
    tm = rows.tile(512)
    tn = _pick(n, 512, V7X_LANES)
    est = 2 * tm * k * 2 + 4 * k * tn * 4 + 2 * k * tn * 2 + 2 * tm * tn * 2 + 2 * tm * tn * 4
    return pl.pallas_call(
        _mm_up_cast_body,
        grid=(n // tn, r // tm),
        in_specs=[
            pl.BlockSpec((tm, k), lambda j, i: (i, 0)),
            _w_spec(w_gate, lead, k, tn),
            _w_spec(w_up, lead, k, tn),
        ],
        out_specs=pl.BlockSpec((tm, tn), lambda j, i: (i, j)),
        out_shape=jax.ShapeDtypeStruct((r, n), BF16),
        scratch_shapes=[pltpu.VMEM((k, tn), BF16), pltpu.VMEM((k, tn), BF16)],
        compiler_params=_cparams(("arbitrary", "arbitrary"), est),
    )(u, w_gate, w_up)


def _mm_res_body(*refs, x_counts, n_h, n_lat_tiles, coef, k):
    row_tile = pl.program_id(1)
    acc, at = None, 0
    for cnt in x_counts:
        x = _read_h(refs[at:at + cnt], row_tile, n_lat_tiles)
        part = jnp.dot(x, refs[at + cnt][...], preferred_element_type=F32)
        acc = part if acc is None else acc + part
        at += cnt + 1
    h_refs = refs[at:at + n_h]
    m_ref, o_ref = refs[at + n_h:]
    gate = m_ref[0, 3 * k + 2:3 * k + 3, :]
    o_ref[...] = _read_h(h_refs, row_tile, n_lat_tiles) + (coef * gate) * acc


def _res_matmul(xs, w, lead, h, mods, k, coef, rows, n_rows):
    n = w.shape[-1]
    tm = rows.tile(512)
    tn = _pick(n, 512, V7X_LANES)
    in_specs, args, est, x_counts = [], [], 0, []
    for p, x in enumerate(xs):
        kk = (x[0] if isinstance(x, tuple) else x).shape[1]
        assert kk * len(xs) == w.shape[-2]
        x_specs, x_args = _h_specs(x, rows, tm, kk, lambda j, i: (i, 0))
        in_specs += x_specs + [_w_spec(w, lead, kk, tn, p)]
        args += x_args + [w]
        x_counts.append(len(x_args))
        est += 2 * len(x_args) * tm * kk * 2 + 2 * kk * tn * 2
    h_specs, h_args = _h_specs(h, rows, tm, tn, lambda j, i: (i, j))
    in_specs += h_specs + [pl.BlockSpec((1, 9, tn), lambda j, i: (rows.set_of_tile(i, tm), 0, j))]
    est += (4 + 2 * len(h_args)) * tm * tn * 4
    return pl.pallas_call(
        functools.partial(_mm_res_body, x_counts=tuple(x_counts), n_h=len(h_args),
                          n_lat_tiles=rows.n_lat // tm, coef=coef, k=k),
        grid=(n // tn, n_rows // tm),
        in_specs=in_specs,
        out_specs=pl.BlockSpec((tm, tn), lambda j, i: (i, j)),
        out_shape=jax.ShapeDtypeStruct((n_rows, n), F32),
        compiler_params=_cparams(("arbitrary", "arbitrary"), est),
    )(*args, *h_args, mods)


def _rope_pairs(t, c, s1, s2):
    half = MLA_ROPE // 2
    return t * c + pltpu.roll(t, V7X_LANES - half, 1) * s1 + pltpu.roll(t, half, 1) * s2


def _lowrank_body(x_ref, g_ref, w_ref, c_ref, s1_ref, s2_ref, o_ref, *, heads, rope):
    x = x_ref[...].astype(F32)
    ms = jnp.mean(x * x, axis=-1, keepdims=True)
    xn = ((x * lax.rsqrt(ms + EPS)) * g_ref[...]).astype(BF16)
    hw = 2 * V7X_LANES
    for h in range(heads):
        acc = jnp.dot(xn, w_ref[:, h * hw:(h + 1) * hw], preferred_element_type=F32)
        if rope:
            o_ref[:, h * hw:h * hw + V7X_LANES] = acc[:, :V7X_LANES].astype(o_ref.dtype)
            t = _rope_pairs(acc[:, V7X_LANES:], c_ref[...], s1_ref[...], s2_ref[...])
            o_ref[:, h * hw + V7X_LANES:(h + 1) * hw] = t.astype(o_ref.dtype)
        else:
            o_ref[:, h * hw:(h + 1) * hw] = acc.astype(o_ref.dtype)


def _lowrank_up(p, col_block, k, gain, w, tables, rows, heads, rope):
    r = p.shape[0]
    n = w.shape[1]
    tm = rows.tile(512)
    est = 2 * tm * k * 2 + 2 * k * n * 2 + 2 * tm * n * 2 + 6 * tm * V7X_LANES * 4 + tm * k * 8 + 4 * tm * 256 * 4
    tab_spec = pl.BlockSpec((tm, V7X_LANES), lambda i: (i, 0))
    return pl.pallas_call(
        functools.partial(_lowrank_body, heads=heads, rope=rope),
        grid=(r // tm,),
        in_specs=[
            pl.BlockSpec((tm, k), lambda i: (i, col_block)),
            pl.BlockSpec((1, k), lambda i: (0, 0)),
            pl.BlockSpec((k, n), lambda i: (0, 0)),
            tab_spec, tab_spec, tab_spec,
        ],
        out_specs=pl.BlockSpec((tm, n), lambda i: (i, 0)),
        out_shape=jax.ShapeDtypeStruct((r, n), BF16),
        compiler_params=_cparams(("arbitrary",), est),
    )(p, gain.reshape(1, k), w, *tables)


def _kr_rope_body(x_ref, c_ref, s1_ref, s2_ref, o_ref):
    t = _rope_pairs(x_ref[...].astype(F32), c_ref[...], s1_ref[...], s2_ref[...])
    o_ref[...] = t.astype(o_ref.dtype)


def _kr_rope(p, col_block, tables, rows):
    r = p.shape[0]
    tm = rows.tile(512)
    spec = pl.BlockSpec((tm, V7X_LANES), lambda i: (i, 0))
    return pl.pallas_call(
        _kr_rope_body,
        grid=(r // tm,),
        in_specs=[pl.BlockSpec((tm, V7X_LANES), lambda i: (i, col_block)), spec, spec, spec],
        out_specs=spec,
        out_shape=jax.ShapeDtypeStruct((r, V7X_LANES), BF16),
        compiler_params=_cparams(("arbitrary",), 16 * tm * V7X_LANES * 4),
    )(p, *tables)


ATTN_VT_ROWS = V7X_LANES + 16


def _attn_body(*refs, n_ctx, n_lat, tk):
    if n_lat:
        (q_ref, kc_ref, vc_ref, krc_ref, kl_ref, vl_ref, krl_ref,
         o_ref, kcat, vt_c, vt_l, acc_scr, s_a, s_b) = refs
    else:
        q_ref, kc_ref, vc_ref, krc_ref, o_ref, kcat, vt_c, acc_scr, s_a = refs
    ln = V7X_LANES
    nt = (((1,), (1,)), ((), ()))

    @pl.when(pl.program_id(2) == 0)
    def _():
        kcat[0:n_ctx, 0:ln] = kc_ref[...]
        kcat[0:n_ctx, ln:2 * ln] = krc_ref[...]
        eye = (lax.broadcasted_iota(jnp.int32, (ln, ln), 0)
               == lax.broadcasted_iota(jnp.int32, (ln, ln), 1)).astype(BF16)

        def ones_rows(n):
            first = lax.broadcasted_iota(jnp.int32, (ATTN_VT_ROWS - ln, n), 0) == 0
            return jnp.where(first, 1.0, 0.0).astype(BF16)

        vt_c[0:ln, :] = lax.dot_general(eye, vc_ref[...], nt, preferred_element_type=F32).astype(BF16)
        vt_c[ln:, :] = ones_rows(n_ctx)
        if n_lat:
            kcat[n_ctx:n_ctx + n_lat, 0:ln] = kl_ref[...]
            kcat[n_ctx:n_ctx + n_lat, ln:2 * ln] = krl_ref[...]
            for c in range(n_lat // tk):
                vt_l[c, 0:ln, :] = lax.dot_general(eye, vl_ref[c * tk:(c + 1) * tk, :], nt,
                                                   preferred_element_type=F32).astype(BF16)
                vt_l[c, ln:, :] = ones_rows(tk)

    q = q_ref[...]
    tq = q.shape[0]
    acc_scr[...] = jnp.zeros(acc_scr.shape, F32)

    def scores(k, dst):
        dst[0:k.shape[0], :] = lax.dot_general(k, q, nt, preferred_element_type=F32)

    def softmax_pv(src, n, v_t, m_prev):
        s = src[0:n, :]
        m_next = jnp.maximum(m_prev, jnp.max(s, axis=0, keepdims=True))
        p = jnp.exp2(s - m_next).astype(BF16)
        alpha = jnp.exp2(m_prev - m_next)
        acc_scr[...] = alpha * acc_scr[...] + jnp.dot(v_t, p, preferred_element_type=F32)
        return m_next

    def lat_keys(c):
        return kcat[pl.ds(pl.multiple_of(n_ctx + c * tk, ln), tk), :]

    m = jnp.full((1, tq), -jnp.inf, F32)
    scores(kcat[0:n_ctx, :], s_a)
    if not n_lat:
        m = softmax_pv(s_a, n_ctx, vt_c[...], m)
    else:
        n_ch = n_lat // tk
        scores(lat_keys(0), s_b)
        m = softmax_pv(s_a, n_ctx, vt_c[...], m)

        def pair(i, m):
            c = 2 * i
            scores(lat_keys(c + 1), s_a)
            m = softmax_pv(s_b, tk, vt_l[c], m)
            scores(lat_keys(c + 2), s_b)
            return softmax_pv(s_a, tk, vt_l[c + 1], m)

        m = lax.fori_loop(0, n_ch // 2 - 1, pair, m)
        scores(lat_keys(n_ch - 1), s_a)
        m = softmax_pv(s_b, tk, vt_l[n_ch - 2], m)
        m = softmax_pv(s_a, tk, vt_l[n_ch - 1], m)
    out = acc_scr[0:ln, :] / acc_scr[ln:ln + 1, :]
    o_ref[...] = out.T.astype(o_ref.dtype)


def _attention(q, kv, kr, rows, heads, latent):
    b, t, tc = rows.B, rows.T, rows.TC
    ln = V7X_LANES
    ctx_blk0 = rows.n_lat // tc
    if latent:
        tq = _pick(t, 1024, ln)
        nq = t // tq
        tk = _pick(t // 2, 1024, ln)
        assert tk >= tc
        n_lat = t
    else:
        tq, nq, tk, n_lat = tc, 1, tc, 0
    q_row = (lambda bb, qi: bb * nq + qi) if latent else (lambda bb, qi: ctx_blk0 * (tc // tq) + bb)
    in_specs = [
        pl.BlockSpec((tq, 2 * ln), lambda bb, h, qi: (q_row(bb, qi), h)),
        pl.BlockSpec((tc, ln), lambda bb, h, qi: (ctx_blk0 + bb, 2 * h)),
        pl.BlockSpec((tc, ln), lambda bb, h, qi: (ctx_blk0 + bb, 2 * h + 1)),
        pl.BlockSpec((tc, ln), lambda bb, h, qi: (ctx_blk0 + bb, 0)),
    ]
    args = [q, kv, kv, kr]
    scratch = [pltpu.VMEM((tc + n_lat, 2 * ln), BF16), pltpu.VMEM((ATTN_VT_ROWS, tc), BF16)]
    if latent:
        in_specs += [
            pl.BlockSpec((t, ln), lambda bb, h, qi: (bb, 2 * h)),
            pl.BlockSpec((t, ln), lambda bb, h, qi: (bb, 2 * h + 1)),
            pl.BlockSpec((t, ln), lambda bb, h, qi: (bb, 0)),
        ]
        args += [kv, kv, kr]
        scratch.append(pltpu.VMEM((n_lat // tk, ATTN_VT_ROWS, tk), BF16))
    scratch.append(pltpu.VMEM((ATTN_VT_ROWS, tq), F32))
    scratch += [pltpu.VMEM((tk, tq), F32)] * (2 if latent else 1)
    nk = tc + n_lat
    est = (2 * tq * 2 * ln * 2 + 6 * tc * ln * 2 + 6 * n_lat * ln * 2 + nk * 3 * ln * 2
           + 2 * tq * ln * 2 + 2 * tq * ln * 4 + 5 * tq * tk * 4)
    return pl.pallas_call(
        functools.partial(_attn_body, n_ctx=tc, n_lat=n_lat, tk=tk),
        grid=(b, heads, nq),
        in_specs=in_specs,
        out_specs=pl.BlockSpec((tq, ln), lambda bb, h, qi: (bb * nq + qi, h)),
        out_shape=jax.ShapeDtypeStruct((b * nq * tq, heads * ln), BF16),
        scratch_shapes=scratch,
        compiler_params=_cparams(("arbitrary", "arbitrary", "arbitrary"), est),
    )(*args)


def _scan_rows(a, bv, reverse):
    n = a.shape[0]
    s = 1
    while s < n:
        a_sh = _shift_rows(a, s, 1.0, reverse)
        b_sh = _shift_rows(bv, s, 0.0, reverse)
        bv = a * b_sh + bv
        a = a * a_sh
        s *= 2
    return a, bv


def _shift_rows(x, s, fill, reverse):
    n = x.shape[0]
    if s % V7X_SUBLANES == 0:
        pad = jnp.full((s,) + x.shape[1:], fill, x.dtype)
        return jnp.concatenate([x[s:], pad], axis=0) if reverse else jnp.concatenate([pad, x[:n - s]], axis=0)
    row = lax.broadcasted_iota(jnp.int32, x.shape, 0)
    if reverse:
        return jnp.where(row < (n - s), pltpu.roll(x, n - s, 0), fill)
    return jnp.where(row >= s, pltpu.roll(x, s, 0), fill)


def _cumsum_rows(x, reverse):
    n = x.shape[0]
    s = 1
    while s < n:
        x = x + _shift_rows(x, s, 0.0, reverse)
        s *= 2
    return x


def _gelu_tanh(x):
    return 0.5 * x * (1.0 + jnp.tanh(math.sqrt(2.0 / math.pi) * (x + 0.044715 * (x * x * x))))


def _lru_body(xl_ref, xc_ref, gl_ref, gc_ref, cw_ref, cb_ref, wa_ref, ba_ref, wi_ref, bi_ref, lam_ref,
              yl_ref, yc_ref, cv_l, cv_c, hf_l, hf_c, *, seq, ctx, chunk, conv_chunk):
    pad = V7X_SUBLANES
    zeros8 = jnp.zeros((pad, V7X_LANES), F32)

    def conv_into(src_ref, dst, n, step):
        for c0 in range(0, n, step):
            lo = zeros8 if c0 == 0 else src_ref[c0 - pad:c0, :].astype(F32)
            hi = zeros8 if c0 + step >= n else src_ref[c0 + step:c0 + step + pad, :].astype(F32)
            ext = jnp.concatenate([lo, src_ref[c0:c0 + step, :].astype(F32), hi], axis=0)
            out = cb_ref[...]
            for kk in range(CONV_W):
                off = pad + kk - CONV_PAD_L
                out = out + ext[off:off + step, :] * cw_ref[kk:kk + 1, :]
            dst[c0:c0 + step, :] = out

    conv_into(xl_ref, cv_l, seq, conv_chunk)
    conv_into(xc_ref, cv_c, ctx, min(ctx, conv_chunk))

    for d, reverse in enumerate((False, True)):
        lam = lam_ref[d:d + 1, :]
        neg_sp = -LRU_C * (jnp.maximum(-lam, 0.0) + jnp.log(1.0 + jnp.exp(-jnp.abs(lam))))
        wa = wa_ref[d, 0]
        wi = wi_ref[d, 0]
        ba = ba_ref[d:d + 1, :]
        bi = bi_ref[d:d + 1, :]

        def block(x, carry):
            xg = x.astype(BF16)
            r = jax.nn.sigmoid(jnp.dot(xg, wa, preferred_element_type=F32) + ba)
            i = jax.nn.sigmoid(jnp.dot(xg, wi, preferred_element_type=F32) + bi)
            a = jnp.exp(neg_sp * r)
            bv = jnp.sqrt(1.0 - a * a) * (i * x)
            a_cum, h = _scan_rows(a, bv, reverse)
            h = a_cum * carry + h
            n = x.shape[0]
            new_carry = h[0:1, :] if reverse else h[n - 1:n, :]
            return h, new_carry

        def emit(dst_ref, fwd_scr, g_ref, start, n, h):
            if not reverse:
                fwd_scr[pl.ds(start, n), :] = h
            else:
                tot = fwd_scr[pl.ds(start, n), :] + h
                g = g_ref[pl.ds(start, n), :].astype(F32)
                dst_ref[pl.ds(start, n), :] = (tot * _gelu_tanh(g)).astype(dst_ref.dtype)

        carry = jnp.zeros((1, V7X_LANES), F32)
        cchunk = min(ctx, chunk)
        n_cc = ctx // cchunk
        order = range(n_cc - 1, -1, -1) if reverse else range(n_cc)
        lat_init = None
        for c in order:
            h, carry = block(cv_c[c * cchunk:(c + 1) * cchunk, :], carry)
            if lat_init is None:
                lat_init = h[cchunk - 1:cchunk, :] if reverse else h[0:1, :]
            emit(yc_ref, hf_c, gc_ref, c * cchunk, cchunk, h)
        carry = lat_init

        n_lc = seq // chunk

        def body(it, carry):
            c = (n_lc - 1 - it) if reverse else it
            start = pl.multiple_of(c * chunk, chunk)
            h, carry = block(cv_l[pl.ds(start, chunk), :], carry)
            emit(yl_ref, hf_l, gl_ref, start, chunk, h)
            return carry

        lax.fori_loop(0, n_lc, body, carry)


def _rglru(p, xb_col, gb_col, width, conv_w, conv_b, w_a, b_a, w_i, b_i, lam, rows):
    b, t, tc = rows.B, rows.T, rows.TC
    ln = V7X_LANES
    groups = width // ln
    ctx_blk0 = rows.n_lat // tc
    chunk = _pick(t, 256, V7X_SUBLANES)
    conv_chunk = _pick(t, 1024, V7X_SUBLANES)
    vec = lambda rws: pl.BlockSpec((rws, ln), lambda bb, g: (0, g))
    est = 2 * (t + tc) * ln * (2 + 2 + 2) + 2 * (t + tc) * ln * 4 + 64 * chunk * ln * 4
    return pl.pallas_call(
        functools.partial(_lru_body, seq=t, ctx=tc, chunk=chunk, conv_chunk=conv_chunk),
        grid=(b, groups),
        in_specs=[
            pl.BlockSpec((t, ln), lambda bb, g: (bb, xb_col + g)),
            pl.BlockSpec((tc, ln), lambda bb, g: (ctx_blk0 + bb, xb_col + g)),
            pl.BlockSpec((t, ln), lambda bb, g: (bb, gb_col + g)),
            pl.BlockSpec((tc, ln), lambda bb, g: (ctx_blk0 + bb, gb_col + g)),
            vec(CONV_W), vec(1),
            pl.BlockSpec((2, 1, ln, ln), lambda bb, g: (0, g, 0, 0)), vec(2),
            pl.BlockSpec((2, 1, ln, ln), lambda bb, g: (0, g, 0, 0)), vec(2),
            vec(2),
        ],
        out_specs=[
            pl.BlockSpec((t, ln), lambda bb, g: (bb, g)),
            pl.BlockSpec((tc, ln), lambda bb, g: (bb, g)),
        ],
        out_shape=[jax.ShapeDtypeStruct((b * t, width), BF16),
                   jax.ShapeDtypeStruct((b * tc, width), BF16)],
        scratch_shapes=[pltpu.VMEM((t, ln), F32), pltpu.VMEM((tc, ln), F32),
                        pltpu.VMEM((t, ln), F32), pltpu.VMEM((tc, ln), F32)],
        compiler_params=_cparams(("arbitrary", "arbitrary"), est),
    )(p, p, p, p, conv_w, conv_b.reshape(1, width), w_a, b_a, w_i, b_i, lam)


def _log_sigmoid(x):
    return jnp.minimum(x, 0.0) - jnp.log(1.0 + jnp.exp(-jnp.abs(x)))


def _linattn_body(*refs, kind, reverse, tb, dk, dv):
    it = iter(refs)
    q_ref, k_ref, v_ref = next(it), next(it), next(it)
    if kind == "ret":
        cos_ref, sin_ref, dec_ref = next(it), next(it), next(it)
    else:
        ga_ref, wg_ref, bg_ref = next(it), next(it), next(it)
    if reverse:
        of_ref, g_ref, gain_ref = next(it), next(it), next(it)
    o_ref, st = next(it), next(it)
    hp = st.shape[0]

    @pl.when(pl.program_id(2) == 0)
    def _():
        st[...] = jnp.zeros(st.shape, F32)

    n_ch = tb // CHUNK
    ri = lax.broadcasted_iota(jnp.int32, (CHUNK, CHUNK), 0)
    ci = lax.broadcasted_iota(jnp.int32, (CHUNK, CHUNK), 1)
    mask = (ci >= ri) if reverse else (ci <= ri)
    nt = (((1,), (1,)), ((), ()))
    tn = (((0,), (0,)), ((), ()))
    pos = lax.broadcasted_iota(jnp.int32, (tb, 1), 0) % CHUNK
    n_terms = ((CHUNK - pos) if reverse else (pos + 1)).astype(F32)
    for hh in range(hp):
        q = q_ref[:, hh * dk:(hh + 1) * dk].astype(F32) * (1.0 if kind == "ret" else dk ** -0.5)
        k = k_ref[:, hh * dk:(hh + 1) * dk].astype(F32) * (dk ** -0.5 if kind == "ret" else 1.0)
        if kind == "ret":
            c, s = cos_ref[...], sin_ref[...]
            hd = dk // 2
            q = jnp.concatenate([q[:, :hd] * c - q[:, hd:] * s, q[:, :hd] * s + q[:, hd:] * c], axis=1)
            k = jnp.concatenate([k[:, :hd] * c - k[:, hd:] * s, k[:, :hd] * s + k[:, hd:] * c], axis=1)
            bcum_all = n_terms * dec_ref[hh]
        else:
            z = jnp.dot(ga_ref[...], wg_ref[0, :, hh * dk:(hh + 1) * dk], preferred_element_type=F32)
            la = _log_sigmoid(z + bg_ref[0, :, hh * dk:(hh + 1) * dk]) * (1.0 / GLA_TAU)
        outs = [None] * n_ch
        for c in (range(n_ch - 1, -1, -1) if reverse else range(n_ch)):
            sl = slice(c * CHUNK, (c + 1) * CHUNK)
            bcum = bcum_all[sl] if kind == "ret" else _cumsum_rows(la[sl], reverse)
            bend = bcum[0:1, :] if reverse else bcum[CHUNK - 1:CHUNK, :]
            qe = (q[sl] * jnp.exp(bcum)).astype(BF16)
            ke = (k[sl] * jnp.exp(-bcum)).astype(BF16)
            ks = (k[sl] * jnp.exp(bend - bcum)).astype(BF16)
            vc = v_ref[sl, hh * dv:(hh + 1) * dv]
            att = lax.dot_general(qe, ke, nt, preferred_element_type=F32)
            att = jnp.where(mask, att, 0.0).astype(BF16)
            o = jnp.dot(att, vc, preferred_element_type=F32)
            o = o + lax.dot_general(qe, st[hh].astype(BF16), nt, preferred_element_type=F32)
            st[hh] = st[hh] * jnp.exp(bend) + lax.dot_general(vc, ks, tn, preferred_element_type=F32)
            outs[c] = o
        o = jnp.concatenate(outs, axis=0)
        cs = slice(hh * dv, (hh + 1) * dv)
        if not reverse:
            o_ref[:, cs] = o
        else:
            y = of_ref[:, cs] + o
            ms = jnp.mean(y * y, axis=-1, keepdims=True)
            y = (y * lax.rsqrt(ms + EPS)) * gain_ref[:, cs]
            g = g_ref[:, cs].astype(F32)
            o_ref[:, cs] = (y * (g * jax.nn.sigmoid(g))).astype(o_ref.dtype)


def _linattn(p, cols, heads, kind, reverse, rows, extra, o_fwd=None, gain=None):
    b, t, tc = rows.B, rows.T, rows.TC
    dk, dv = RET_DK, RET_DV
    tb = tc
    assert t % tb == 0 and tb % CHUNK == 0
    nt = 1 + t // tb
    ctx_blk0 = rows.n_lat // tb
    per_seq = t // tb

    def row_blk(bb, tt):
        lat = (per_seq - tt) if reverse else (tt - 1)
        return jnp.where(tt == 0, ctx_blk0 + bb, bb * per_seq + lat)

    qc, kc, vc, gc = cols
    hp = math.gcd(heads, LINATTN_HEADS_PER_STEP)
    wk, wv = hp * dk, hp * dv
    assert all(o % wk == 0 for o in (qc, kc)) and all(o % wv == 0 for o in (vc, gc))
    in_specs = [
        pl.BlockSpec((tb, wk), lambda bb, h, tt: (row_blk(bb, tt), qc // wk + h)),
        pl.BlockSpec((tb, wk), lambda bb, h, tt: (row_blk(bb, tt), kc // wk + h)),
        pl.BlockSpec((tb, wv), lambda bb, h, tt: (row_blk(bb, tt), vc // wv + h)),
    ]
    args = [p, p, p]
    if kind == "ret":
        cos, sin, dec = extra
        tab = pl.BlockSpec((tb, dk // 2), lambda bb, h, tt: (row_blk(bb, tt), 0))
        in_specs += [tab, tab, pl.BlockSpec((hp, 1, dk), lambda bb, h, tt: (h, 0, 0))]
        args += [cos, sin, dec]
    else:
        ga_rows, wg, bg = extra
        in_specs += [
            pl.BlockSpec((tb, V7X_LANES), lambda bb, h, tt: (row_blk(bb, tt), 0)),
            pl.BlockSpec((1, V7X_LANES, wk), lambda bb, h, tt: (0, 0, h)),
            pl.BlockSpec((1, 1, wk), lambda bb, h, tt: (0, 0, h)),
        ]
        args += [ga_rows, wg, bg]
    if reverse:
        in_specs += [
            pl.BlockSpec((tb, wv), lambda bb, h, tt: (row_blk(bb, tt), h)),
            pl.BlockSpec((tb, wv), lambda bb, h, tt: (row_blk(bb, tt), gc // wv + h)),
            pl.BlockSpec((1, wv), lambda bb, h, tt: (0, h)),
        ]
        args += [o_fwd, p, gain]
    est = hp * (2 * tb * (2 * dk + dv) * 2 + 6 * tb * dv * 4 + 3 * dv * dk * 4 + 24 * tb * dk * 4 + 4 * tb * dv * 4)
    return pl.pallas_call(
        functools.partial(_linattn_body, kind=kind, reverse=reverse, tb=tb, dk=dk, dv=dv),
        grid=(b, heads // hp, nt),
        in_specs=in_specs,
        out_specs=pl.BlockSpec((tb, wv), lambda bb, h, tt: (row_blk(bb, tt), h)),
        out_shape=jax.ShapeDtypeStruct((rows.R, heads * dv), BF16 if reverse else F32),
        scratch_shapes=[pltpu.VMEM((hp, dv, dk), F32)],
        compiler_params=_cparams(("arbitrary", "arbitrary", "arbitrary"), est),
    )(*args)


def _pad_cols(w, n):
    return jnp.pad(w, ((0, 0), (0, n - w.shape[1])))


def _axial_angles(t, rot_dim):
    n_rows = t // GRID_W
    row = jnp.repeat(jnp.arange(n_rows, dtype=F32), GRID_W)
    col = jnp.tile(jnp.arange(GRID_W, dtype=F32), n_rows)
    n_freq = rot_dim // 4
    inv = ROPE_BASE ** (-jnp.arange(n_freq, dtype=F32) / n_freq)
    return jnp.concatenate([row[:, None] * inv, col[:, None] * inv], axis=-1)


def _row_tables(rows, rot_dim):
    ang = _axial_angles(rows.T, rot_dim)
    half = rot_dim // 2
    cos = jnp.concatenate([jnp.tile(jnp.cos(ang), (rows.B, 1)), jnp.ones((rows.n_ctx, half), F32)], axis=0)
    sin = jnp.concatenate([jnp.tile(jnp.sin(ang), (rows.B, 1)), jnp.zeros((rows.n_ctx, half), F32)], axis=0)
    return cos, sin


def _mla_tables(rows):
    cos, sin = _row_tables(rows, MLA_ROPE)
    half = MLA_ROPE // 2
    z = jnp.zeros((rows.R, V7X_LANES - MLA_ROPE), F32)
    zh = jnp.zeros((rows.R, half), F32)
    c = jnp.concatenate([cos, cos, z], axis=1)
    s1 = jnp.concatenate([-sin, zh, z], axis=1)
    s2 = jnp.concatenate([zh, sin, z], axis=1)
    return c, s1, s2


def _even_mixer(h, gain, mods, rows, n_out, w_in, q_norm, w_uq, kv_norm, w_ukv, conv_w, conv_b,
                w_a, b_a, w_i, b_i, lam, w_out, tables):
    q_rank, kv_rank, lru_w = q_norm.shape[0], kv_norm.shape[0], conv_w.shape[1]
    ln = V7X_LANES
    o_cq, o_ckv, o_kr, o_xb, o_gb = np.cumsum([0, q_rank, kv_rank, MLA_ROPE, lru_w]).tolist()
    w_main = jnp.concatenate([w_in[:, :o_kr], w_in[:, o_xb:]], axis=1).astype(BF16)
    w_kr = _pad_cols(w_in[:, o_kr:o_xb], ln).astype(BF16)
    p, kr_raw = _norm_matmul(h, gain, mods, 1, w_main, w_main.shape[1], w_kr, rows, 4 * V7X_MXU_DIM)
    c_cq, c_ckv, c_xb, c_gb = 0, q_rank, q_rank + kv_rank, q_rank + kv_rank + lru_w

    scale = (MLA_NOPE + MLA_ROPE) ** -0.5 * math.log2(math.e)
    wq = (w_uq * scale).reshape(q_rank, MLA_HEADS, MLA_NOPE + MLA_ROPE)
    wq = jnp.pad(wq, ((0, 0), (0, 0), (0, 2 * ln - MLA_NOPE - MLA_ROPE))).reshape(q_rank, MLA_HEADS * 2 * ln)
    assert c_cq % q_rank == 0 and c_ckv % kv_rank == 0
    q = _lowrank_up(p, c_cq // q_rank, q_rank, q_norm, wq.astype(BF16), tables, rows, MLA_HEADS, True)
    kv = _lowrank_up(p, c_ckv // kv_rank, kv_rank, kv_norm, w_ukv.astype(BF16), tables, rows, MLA_HEADS, False)
    kr = _kr_rope(kr_raw, 0, tables, rows)

    att = _attention(q, kv, kr, rows, MLA_HEADS, latent=True)
    att_c = _attention(q, kv, kr, rows, MLA_HEADS, latent=False)

    y_l, y_c = _rglru(p, c_xb // ln, c_gb // ln, lru_w, conv_w, conv_b,
                      w_a.astype(BF16), b_a, w_i.astype(BF16), b_i, lam, rows)
    n_att = MLA_HEADS * MLA_V
    assert MLA_HEADS * MLA_V == lru_w
    return _res_matmul([(att, att_c), (y_l, y_c)], w_out.astype(BF16), (), h, mods, 1, 1.0, rows, n_out)


def _odd_mixer(h, gain, mods, rows, n_out, w_in, ret_log_decay, ret_norm, gla_w_gate2, gla_b_gate, gla_norm,
               w_out, tables):
    ln = V7X_LANES
    sizes = [RET_HEADS * RET_DK, RET_HEADS * RET_DK, RET_HEADS * RET_DV, RET_HEADS * RET_DV,
             GLA_HEADS * GLA_DK, GLA_HEADS * GLA_DK, GLA_HEADS * GLA_DV, GLA_HEADS * GLA_DV]
    offs = np.cumsum([0] + sizes).tolist()
    rq, rk, rv, rg, gq, gk, gv, gr, ga = offs
    w_main = w_in.astype(BF16)
    w_ga = _pad_cols(w_in[:, ga:], ln).astype(BF16)
    p, ga_rows = _norm_matmul(h, gain, mods, 1, w_main, ga, w_ga, rows, 4 * V7X_MXU_DIM)

    cos, sin = tables
    ys = []
    for kind, heads, cols, norm in (("ret", RET_HEADS, (rq, rk, rv, rg), ret_norm),
                                    ("gla", GLA_HEADS, (gq, gk, gv, gr), gla_norm)):
        o_f = None
        for d, reverse in enumerate((False, True)):
            if kind == "ret":
                dec = jnp.broadcast_to(ret_log_decay[d][:, None, None], (heads, 1, RET_DK)).astype(F32)
                extra = (cos, sin, dec)
            else:
                wg = jnp.zeros((1, ln, heads * GLA_DK), F32).at[0, d * GLA_RANK:(d + 1) * GLA_RANK].set(gla_w_gate2[d])
                extra = (ga_rows, wg.astype(BF16), gla_b_gate[d].reshape(1, 1, heads * GLA_DK))
            out = _linattn(p, cols, heads, kind, reverse, rows, extra, o_fwd=o_f,
                           gain=norm.reshape(1, heads * RET_DV))
            if reverse:
                ys.append(out)
            else:
                o_f = out
    n_ret = RET_HEADS * RET_DV
    assert RET_HEADS * RET_DV == GLA_HEADS * GLA_DV
    return _res_matmul(ys, w_out.astype(BF16), (), h, mods, 1, 1.0, rows, n_out)


def kernel(x, c, ctx, c_ctx, ada_w, ada_b, norm_w, ffn_w_gate, ffn_w_up, ffn_w_down, ev_w_in, mla_q_norm, mla_w_uq, mla_kv_norm, mla_w_ukv, lru_conv_w, lru_conv_b, lru_w_a, lru_b_a, lru_w_i, lru_b_i, lru_lambda, ev_w_out, od_w_in, ret_log_decay, ret_norm, gla_w_gate2, gla_b_gate, gla_norm, od_w_out, final_norm_w):
    batch, seq, d = x.shape
    tc = ctx.shape[1]
    depth = ada_w.shape[0]
    rows = _Rows(batch, seq, tc)
    h = (x.reshape(batch * seq, d), ctx.reshape(batch * tc, d))

    n_sets = 1 + batch
    c_rows = jnp.concatenate([c_ctx[None, :], c], axis=0)
    c_rows = jnp.pad(c_rows, ((0, -n_sets % V7X_SUBLANES), (0, 0)))
    mods_all = _ada(c_rows, ada_w, ada_b)[:, :n_sets].reshape(depth, n_sets, 9, d)

    mla_tabs = _mla_tables(rows)
    ret_tabs = _row_tables(rows, RET_DK)
    w_down_bf = ffn_w_down.astype(BF16)

    for l in range(depth):
        mods = mods_all[l]
        n_out = rows.n_lat if l == depth - 1 else rows.R

        def ffn(hh, k, idx, n_rows):
            u = _norm_mod(hh, norm_w[l, k], mods, k, rows, BF16, n_rows)
            a = _ffn_up(u, ffn_w_gate, ffn_w_up, (l, idx), rows)
            return _res_matmul([a], w_down_bf, (l, idx), hh, mods, k, FFN_RES, rows, n_rows)

        h = ffn(h, 0, 0, rows.R)
        if l % 2 == 0:
            e = l // 2
            h = _even_mixer(h, norm_w[l, 1], mods, rows, n_out, ev_w_in[e], mla_q_norm[e], mla_w_uq[e], mla_kv_norm[e],
                            mla_w_ukv[e], lru_conv_w[e], lru_conv_b[e], lru_w_a[e], lru_b_a[e],
                            lru_w_i[e], lru_b_i[e], lru_lambda[e], ev_w_out[e], mla_tabs)
        else:
            o = l // 2
            h = _odd_mixer(h, norm_w[l, 1], mods, rows, n_out, od_w_in[o], ret_log_decay[o], ret_norm[o], gla_w_gate2[o],
                           gla_b_gate[o], gla_norm[o], od_w_out[o], ret_tabs)
        h = ffn(h, 2, 1, n_out)

    out = _norm_mod(h, final_norm_w, mods_all[0], None, rows, F32, rows.n_lat)
    return out.reshape(batch, seq, d)
```

```python
import functools
import math

import jax
import jax.numpy as jnp
import numpy as np
from jax import lax
from jax.experimental import pallas as pl
from jax.experimental.pallas import tpu as pltpu

F32 = jnp.float32
BF16 = jnp.bfloat16

V7X_LANES = 128
V7X_SUBLANES = 8
V7X_MXU_DIM = 256
V7X_VMEM_BYTES = 64 * 1024 * 1024
V7X_VMEM_CEILING = 58 * 1024 * 1024

GRID_W = 64
EPS = 1e-6
ROPE_BASE = 10000.0
FFN_RES = 0.5
MLA_HEADS = 16
MLA_NOPE = 128
MLA_ROPE = 64
MLA_V = 128
LRU_BLOCKS = 16
LRU_C = 8.0
CONV_W = 4
CONV_PAD_L = 2
RET_HEADS = 4
RET_DK = 256
RET_DV = 512
GLA_HEADS = 4
GLA_DK = 256
GLA_DV = 512
GLA_RANK = 16
GLA_TAU = 16.0
CHUNK = 64
LINATTN_HEADS_PER_STEP = 4


def _cparams(sem, vmem_est):
    limit = int(min(V7X_VMEM_CEILING, max(vmem_est * 5 // 4 + (4 << 20), 16 << 20)))
    return pltpu.CompilerParams(dimension_semantics=sem, vmem_limit_bytes=limit)


def _pick(n, cap, mult):
    best = None
    for t in range(mult, min(n, cap) + 1, mult):
        if n % t == 0:
            best = t
    assert best is not None, (n, cap, mult)
    return best


class _Rows:
    def __init__(self, batch, seq, ctx):
        self.B, self.T, self.TC = batch, seq, ctx
        self.n_lat = batch * seq
        self.n_ctx = batch * ctx
        self.R = self.n_lat + self.n_ctx

    def tile(self, cap):
        g = math.gcd(self.T, self.n_ctx)
        return _pick(g, cap, V7X_SUBLANES)

    def set_of_tile(self, i, tm):
        n_lat_tiles = self.n_lat // tm
        per_seq = self.T // tm
        return jnp.where(i < n_lat_tiles, 1 + i // per_seq, 0)


def _ada_body(c_ref, w_ref, b_ref, o_ref):
    s = c_ref[...]
    s = (s * jax.nn.sigmoid(s)).astype(BF16)
    w = w_ref[0].astype(BF16)
    o_ref[0] = jnp.dot(s, w, preferred_element_type=F32) + b_ref[0]


def _ada(c_rows, ada_w, ada_b):
    depth, d, n = ada_w.shape
    rows = c_rows.shape[0]
    tn = _pick(n, 512, V7X_LANES)
    est = 2 * d * tn * 4 + d * tn * 2 + 4 * rows * d * 4
    return pl.pallas_call(
        _ada_body,
        grid=(depth, n // tn),
        in_specs=[
            pl.BlockSpec((rows, d), lambda l, j: (0, 0)),
            pl.BlockSpec((1, d, tn), lambda l, j: (l, 0, j)),
            pl.BlockSpec((1, 1, tn), lambda l, j: (l, 0, j)),
        ],
        out_specs=pl.BlockSpec((1, rows, tn), lambda l, j: (l, 0, j)),
        out_shape=jax.ShapeDtypeStruct((depth, rows, n), F32),
        compiler_params=_cparams(("arbitrary", "arbitrary"), est),
    )(c_rows, ada_w, ada_b.reshape(depth, 1, n))


def _h_specs(h, rows, tm, tn, tile_of):
    if not isinstance(h, tuple):
        return [pl.BlockSpec((tm, tn), tile_of)], [h]
    n_lat_tiles = rows.n_lat // tm

    def lat(*g):
        i, j = tile_of(*g)
        return jnp.minimum(i, n_lat_tiles - 1), j

    def ctx(*g):
        i, j = tile_of(*g)
        return jnp.maximum(i - n_lat_tiles, 0), j

    return [pl.BlockSpec((tm, tn), lat), pl.BlockSpec((tm, tn), ctx)], list(h)


def _read_h(h_refs, row_tile, n_lat_tiles):
    if len(h_refs) == 1:
        return h_refs[0][...]
    return jnp.where(row_tile < n_lat_tiles, h_refs[0][...], h_refs[1][...])


def _norm_body(*refs, k, n_h, n_lat_tiles):
    g_ref, m_ref, o_ref = refs[n_h:]
    x = _read_h(refs[:n_h], pl.program_id(0), n_lat_tiles)
    ms = jnp.mean(x * x, axis=-1, keepdims=True)
    y = (x * lax.rsqrt(ms + EPS)) * g_ref[...]
    if k is not None:
        y = y * (1.0 + m_ref[0, 3 * k + 1:3 * k + 2, :]) + m_ref[0, 3 * k:3 * k + 1, :]
    o_ref[...] = y.astype(o_ref.dtype)


def _norm_mod(h, gain, mods, k, rows, out_dtype, n_rows):
    d = gain.shape[0]
    tm = rows.tile(256 if isinstance(h, tuple) else 512)
    h_specs, h_args = _h_specs(h, rows, tm, d, lambda i: (i, 0))
    est = 2 * tm * d * (4 * len(h_args) + 4) + 2 * tm * d * 4
    return pl.pallas_call(
        functools.partial(_norm_body, k=k, n_h=len(h_args), n_lat_tiles=rows.n_lat // tm),
        grid=(n_rows // tm,),
        in_specs=h_specs + [
            pl.BlockSpec((1, d), lambda i: (0, 0)),
            pl.BlockSpec((1, 9, d), lambda i: (rows.set_of_tile(i, tm), 0, 0)),
        ],
        out_specs=pl.BlockSpec((tm, d), lambda i: (i, 0)),
        out_shape=jax.ShapeDtypeStruct((n_rows, d), out_dtype),
        compiler_params=_cparams(("arbitrary",), est),
    )(*h_args, gain.reshape(1, d), mods)


def _w_spec(w, lead, kk, tn, row_block=0):
    assert w.ndim == len(lead) + 2
    return pl.BlockSpec((None,) * len(lead) + (kk, tn), lambda j, i: tuple(lead) + (row_block, j))


NORM_ROWS_PER_PASS = 64


def _norm_into(u_scr, h_ref, g_ref, m_ref, gs_scr, k):
    rows, d = u_scr.shape
    ln, sub = V7X_LANES, V7X_SUBLANES
    step = min(NORM_ROWS_PER_PASS, rows)
    gs_scr[0] = jnp.broadcast_to(g_ref[...] * (1.0 + m_ref[0, 3 * k + 1:3 * k + 2, :]), (sub, d))
    gs_scr[1] = jnp.broadcast_to(m_ref[0, 3 * k:3 * k + 1, :], (sub, d))

    def rows_pass(it, carry):
        r0 = pl.multiple_of(it * step, step)
        acc = jnp.zeros((step, ln), F32)
        for c in range(0, d, ln):
            x = h_ref[pl.ds(r0, step), c:c + ln]
            acc = acc + x * x
        ms = jnp.sum(acc, axis=-1, keepdims=True) * (1.0 / d)
        rinv = jnp.broadcast_to(lax.rsqrt(ms + EPS), (step, ln))
        for c in range(0, d, ln):
            x = (h_ref[pl.ds(r0, step), c:c + ln] * rinv).reshape(step // sub, sub, ln)
            y = x * gs_scr[0, :, c:c + ln] + gs_scr[1, :, c:c + ln]
            u_scr[pl.ds(r0, step), c:c + ln] = y.reshape(step, ln).astype(u_scr.dtype)
        return carry

    lax.fori_loop(0, rows // step, rows_pass, 0)


def _mm_proj_body(h_ref, g_ref, m_ref, w_ref, ws_ref, o_ref, os_ref, u_scr, gs_scr, *, k):
    @pl.when(pl.program_id(1) == 0)
    def _():
        _norm_into(u_scr, h_ref, g_ref, m_ref, gs_scr, k)
        os_ref[...] = jnp.dot(u_scr[...], ws_ref[...], preferred_element_type=F32).astype(os_ref.dtype)

    o_ref[...] = jnp.dot(u_scr[...], w_ref[...], preferred_element_type=F32).astype(o_ref.dtype)


def _norm_matmul(h, gain, mods, k, w, n, w_side, rows, tn_cap):
    r, d = h.shape
    assert n <= w.shape[-1]
    ns = w_side.shape[-1]
    tm = rows.tile(512)
    tn = _pick(n, tn_cap, V7X_LANES)
    est = (2 * tm * d * 4 + tm * d * 2 + 2 * d * (tn + ns) * 2 + 2 * tm * (tn + ns) * 2
           + 2 * tm * tn * 4 + 6 * NORM_ROWS_PER_PASS * d * 4)
    return pl.pallas_call(
        functools.partial(_mm_proj_body, k=k),
        grid=(r // tm, n // tn),
        in_specs=[
            pl.BlockSpec((tm, d), lambda i, j: (i, 0)),
            pl.BlockSpec((1, d), lambda i, j: (0, 0)),
            pl.BlockSpec((1, 9, d), lambda i, j: (rows.set_of_tile(i, tm), 0, 0)),
            pl.BlockSpec((d, tn), lambda i, j: (0, j)),
            pl.BlockSpec((d, ns), lambda i, j: (0, 0)),
        ],
        out_specs=[pl.BlockSpec((tm, tn), lambda i, j: (i, j)),
                   pl.BlockSpec((tm, ns), lambda i, j: (i, 0))],
        out_shape=[jax.ShapeDtypeStruct((r, n), BF16), jax.ShapeDtypeStruct((r, ns), BF16)],
        scratch_shapes=[pltpu.VMEM((tm, d), BF16), pltpu.VMEM((2, V7X_SUBLANES, d), F32)],
        compiler_params=_cparams(("arbitrary", "arbitrary"), est),
    )(h, gain.reshape(1, d), mods, w, w_side)


CAST_ROWS_PER_PASS = 512


def _mm_up_cast_body(x_ref, wg_ref, wu_ref, o_ref, wg_scr, wu_scr):
    @pl.when(pl.program_id(1) == 0)
    def _():
        kk = wg_scr.shape[0]
        step = min(CAST_ROWS_PER_PASS, kk)
        for r0 in range(0, kk, step):
            wg_scr[r0:r0 + step, :] = wg_ref[r0:r0 + step, :].astype(BF16)
            wu_scr[r0:r0 + step, :] = wu_ref[r0:r0 + step, :].astype(BF16)

    x = x_ref[...]
    g = jnp.dot(x, wg_scr[...], preferred_element_type=F32)
    u = jnp.dot(x, wu_scr[...], preferred_element_type=F32)
    o_ref[...] = (g * jax.nn.sigmoid(g) * u).astype(o_ref.dtype)


def _ffn_up(u, w_gate, w_up, lead, rows):
    r, k = u.shape
    n = w_gate.shape[-1]
    tm = rows.tile(512)
    tn = _pick(n, 512, V7X_LANES)
    est = 2 * tm * k * 2 + 4 * k * tn * 4 + 2 * k * tn * 2 + 2 * tm * tn * 2 + 2 * tm * tn * 4
    return pl.pallas_call(
        _mm_up_cast_body,
        grid=(n // tn, r // tm),
        in_specs=[
            pl.BlockSpec((tm, k), lambda j, i: (i, 0)),
            _w_spec(w_gate, lead, k, tn),
            _w_spec(w_up, lead, k, tn),
        ],
        out_specs=pl.BlockSpec((tm, tn), lambda j, i: (i, j)),
        out_shape=jax.ShapeDtypeStruct((r, n), BF16),
        scratch_shapes=[pltpu.VMEM((k, tn), BF16), pltpu.VMEM((k, tn), BF16)],
        compiler_params=_cparams(("arbitrary", "arbitrary"), est),
    )(u, w_gate, w_up)


def _mm_res_body(*refs, x_counts, n_h, n_lat_tiles, coef, k):
    row_tile = pl.program_id(1)
    acc, at = None, 0
    for cnt in x_counts:
        x = _read_h(refs[at:at + cnt], row_tile, n_lat_tiles)
        part = jnp.dot(x, refs[at + cnt][...], preferred_element_type=F32)
        acc = part if acc is None else acc + part
        at += cnt + 1
    h_refs = refs[at:at + n_h]
    m_ref, o_ref = refs[at + n_h:]
    gate = m_ref[0, 3 * k + 2:3 * k + 3, :]
    o_ref[...] = _read_h(h_refs, row_tile, n_lat_tiles) + (coef * gate) * acc


def _res_matmul(xs, w, lead, h, mods, k, coef, rows, n_rows):
    n = w.shape[-1]
    tm = rows.tile(512)
    tn = _pick(n, 512, V7X_LANES)
    in_specs, args, est, x_counts = [], [], 0, []
    for p, x in enumerate(xs):
        kk = (x[0] if isinstance(x, tuple) else x).shape[1]
        assert kk * len(xs) == w.shape[-2]
        x_specs, x_args = _h_specs(x, rows, tm, kk, lambda j, i: (i, 0))
        in_specs += x_specs + [_w_spec(w, lead, kk, tn, p)]
        args += x_args + [w]
        x_counts.append(len(x_args))
        est += 2 * len(x_args) * tm * kk * 2 + 2 * kk * tn * 2
    h_specs, h_args = _h_specs(h, rows, tm, tn, lambda j, i: (i, j))
    in_specs += h_specs + [pl.BlockSpec((1, 9, tn), lambda j, i: (rows.set_of_tile(i, tm), 0, j))]
    est += (4 + 2 * len(h_args)) * tm * tn * 4
    return pl.pallas_call(
        functools.partial(_mm_res_body, x_counts=tuple(x_counts), n_h=len(h_args),
                          n_lat_tiles=rows.n_lat // tm, coef=coef, k=k),
        grid=(n // tn, n_rows // tm),
        in_specs=in_specs,
        out_specs=pl.BlockSpec((tm, tn), lambda j, i: (i, j)),
        out_shape=jax.ShapeDtypeStruct((n_rows, n), F32),
        compiler_params=_cparams(("arbitrary", "arbitrary"), est),
    )(*args, *h_args, mods)


def _rope_pairs(t, c, s1, s2):
    half = MLA_ROPE // 2
    return t * c + pltpu.roll(t, V7X_LANES - half, 1) * s1 + pltpu.roll(t, half, 1) * s2


def _lowrank_body(x_ref, g_ref, w_ref, c_ref, s1_ref, s2_ref, o_ref, *, heads, rope):
    x = x_ref[...].astype(F32)
    ms = jnp.mean(x * x, axis=-1, keepdims=True)
    xn = ((x * lax.rsqrt(ms + EPS)) * g_ref[...]).astype(BF16)
    hw = 2 * V7X_LANES
    for h in range(heads):
        acc = jnp.dot(xn, w_ref[:, h * hw:(h + 1) * hw], preferred_element_type=F32)
        if rope:
            o_ref[:, h * hw:h * hw + V7X_LANES] = acc[:, :V7X_LANES].astype(o_ref.dtype)
            t = _rope_pairs(acc[:, V7X_LANES:], c_ref[...], s1_ref[...], s2_ref[...])
            o_ref[:, h * hw + V7X_LANES:(h + 1) * hw] = t.astype(o_ref.dtype)
        else:
            o_ref[:, h * hw:(h + 1) * hw] = acc.astype(o_ref.dtype)


def _lowrank_up(p, col_block, k, gain, w, tables, rows, heads, rope):
    r = p.shape[0]
    n = w.shape[1]
    tm = rows.tile(512)
    est = 2 * tm * k * 2 + 2 * k * n * 2 + 2 * tm * n * 2 + 6 * tm * V7X_LANES * 4 + tm * k * 8 + 4 * tm * 256 * 4
    tab_spec = pl.BlockSpec((tm, V7X_LANES), lambda i: (i, 0))
    return pl.pallas_call(
        functools.partial(_lowrank_body, heads=heads, rope=rope),
        grid=(r // tm,),
        in_specs=[
            pl.BlockSpec((tm, k), lambda i: (i, col_block)),
            pl.BlockSpec((1, k), lambda i: (0, 0)),
            pl.BlockSpec((k, n), lambda i: (0, 0)),
            tab_spec, tab_spec, tab_spec,
        ],
        out_specs=pl.BlockSpec((tm, n), lambda i: (i, 0)),
        out_shape=jax.ShapeDtypeStruct((r, n), BF16),
        compiler_params=_cparams(("arbitrary",), est),
    )(p, gain.reshape(1, k), w, *tables)


def _kr_rope_body(x_ref, c_ref, s1_ref, s2_ref, o_ref):
    t = _rope_pairs(x_ref[...].astype(F32), c_ref[...], s1_ref[...], s2_ref[...])
    o_ref[...] = t.astype(o_ref.dtype)


def _kr_rope(p, col_block, tables, rows):
    r = p.shape[0]
    tm = rows.tile(512)
    spec = pl.BlockSpec((tm, V7X_LANES), lambda i: (i, 0))
    return pl.pallas_call(
        _kr_rope_body,
        grid=(r // tm,),
        in_specs=[pl.BlockSpec((tm, V7X_LANES), lambda i: (i, col_block)), spec, spec, spec],
        out_specs=spec,
        out_shape=jax.ShapeDtypeStruct((r, V7X_LANES), BF16),
        compiler_params=_cparams(("arbitrary",), 16 * tm * V7X_LANES * 4),
    )(p, *tables)


def _attn_body(*refs, n_ctx, n_lat, tk):
    if n_lat:
        (q_ref, kc_ref, vc_ref, krc_ref, kl_ref, vl_ref, krl_ref,
         o_ref, kcat, vt_c, vt_l, acc_scr, s_a, s_b) = refs
    else:
        q_ref, kc_ref, vc_ref, krc_ref, o_ref, kcat, vt_c, acc_scr, s_a = refs
    ln = V7X_LANES
    nt = (((1,), (1,)), ((), ()))

    @pl.when(pl.program_id(2) == 0)
    def _():
        kcat[0:n_ctx, 0:ln] = kc_ref[...]
        kcat[0:n_ctx, ln:2 * ln] = krc_ref[...]
        eye = (lax.broadcasted_iota(jnp.int32, (ln, ln), 0)
               == lax.broadcasted_iota(jnp.int32, (ln, ln), 1)).astype(BF16)
        vt_c[...] = lax.dot_general(eye, vc_ref[...], nt, preferred_element_type=F32).astype(BF16)
        if n_lat:
            kcat[n_ctx:n_ctx + n_lat, 0:ln] = kl_ref[...]
            kcat[n_ctx:n_ctx + n_lat, ln:2 * ln] = krl_ref[...]
            for c in range(n_lat // tk):
                vt_l[c] = lax.dot_general(eye, vl_ref[c * tk:(c + 1) * tk, :], nt,
                                          preferred_element_type=F32).astype(BF16)

    q = q_ref[...]
    tq = q.shape[0]
    acc_scr[...] = jnp.zeros(acc_scr.shape, F32)

    def scores(k, dst):
        dst[0:k.shape[0], :] = lax.dot_general(k, q, nt, preferred_element_type=F32)

    def softmax_pv(src, n, v_t, stats):
        m_prev, l_prev = stats
        s = src[0:n, :]
        m_next = jnp.maximum(m_prev, jnp.max(s, axis=0, keepdims=True))
        p = jnp.exp2(s - m_next)
        alpha = jnp.exp2(m_prev - m_next)
        l_next = alpha * l_prev + jnp.sum(p, axis=0, keepdims=True)
        acc_scr[...] = alpha * acc_scr[...] + jnp.dot(v_t, p.astype(BF16), preferred_element_type=F32)
        return m_next, l_next

    def lat_keys(c):
        return kcat[pl.ds(pl.multiple_of(n_ctx + c * tk, ln), tk), :]

    m = (jnp.full((1, tq), -jnp.inf, F32), jnp.zeros((1, tq), F32))
    scores(kcat[0:n_ctx, :], s_a)
    if not n_lat:
        m = softmax_pv(s_a, n_ctx, vt_c[...], m)
    else:
        n_ch = n_lat // tk
        scores(lat_keys(0), s_b)
        m = softmax_pv(s_a, n_ctx, vt_c[...], m)

        def pair(i, m):
            c = 2 * i
            scores(lat_keys(c + 1), s_a)
            m = softmax_pv(s_b, tk, vt_l[c], m)
            scores(lat_keys(c + 2), s_b)
            return softmax_pv(s_a, tk, vt_l[c + 1], m)

        m = lax.fori_loop(0, n_ch // 2 - 1, pair, m)
        scores(lat_keys(n_ch - 1), s_a)
        m = softmax_pv(s_b, tk, vt_l[n_ch - 2], m)
        m = softmax_pv(s_a, tk, vt_l[n_ch - 1], m)
    o_ref[...] = (acc_scr[...] / m[1]).T.astype(o_ref.dtype)


def _attention(q, kv, kr, rows, heads, latent):
    b, t, tc = rows.B, rows.T, rows.TC
    ln = V7X_LANES
    ctx_blk0 = rows.n_lat // tc
    if latent:
        tq = _pick(t, 1024, ln)
        nq = t // tq
        tk = _pick(t // 2, 1024, ln)
        assert tk >= tc
        n_lat = t
    else:
        tq, nq, tk, n_lat = tc, 1, tc, 0
    q_row = (lambda bb, qi: bb * nq + qi) if latent else (lambda bb, qi: ctx_blk0 * (tc // tq) + bb)
    in_specs = [
        pl.BlockSpec((tq, 2 * ln), lambda bb, h, qi: (q_row(bb, qi), h)),
        pl.BlockSpec((tc, ln), lambda bb, h, qi: (ctx_blk0 + bb, 2 * h)),
        pl.BlockSpec((tc, ln), lambda bb, h, qi: (ctx_blk0 + bb, 2 * h + 1)),
        pl.BlockSpec((tc, ln), lambda bb, h, qi: (ctx_blk0 + bb, 0)),
    ]
    args = [q, kv, kv, kr]
    scratch = [pltpu.VMEM((tc + n_lat, 2 * ln), BF16), pltpu.VMEM((ln, tc), BF16)]
    if latent:
        in_specs += [
            pl.BlockSpec((t, ln), lambda bb, h, qi: (bb, 2 * h)),
            pl.BlockSpec((t, ln), lambda bb, h, qi: (bb, 2 * h + 1)),
            pl.BlockSpec((t, ln), lambda bb, h, qi: (bb, 0)),
        ]
        args += [kv, kv, kr]
        scratch.append(pltpu.VMEM((n_lat // tk, ln, tk), BF16))
    scratch.append(pltpu.VMEM((ln, tq), F32))
    scratch += [pltpu.VMEM((tk, tq), F32)] * (2 if latent else 1)
    nk = tc + n_lat
    est = (2 * tq * 2 * ln * 2 + 6 * tc * ln * 2 + 6 * n_lat * ln * 2 + nk * 3 * ln * 2
           + 2 * tq * ln * 2 + 2 * tq * ln * 4 + 5 * tq * tk * 4)
    return pl.pallas_call(
        functools.partial(_attn_body, n_ctx=tc, n_lat=n_lat, tk=tk),
        grid=(b, heads, nq),
        in_specs=in_specs,
        out_specs=pl.BlockSpec((tq, ln), lambda bb, h, qi: (bb * nq + qi, h)),
        out_shape=jax.ShapeDtypeStruct((b * nq * tq, heads * ln), BF16),
        scratch_shapes=scratch,
        compiler_params=_cparams(("arbitrary", "arbitrary", "arbitrary"), est),
    )(*args)


def _scan_rows(a, bv, reverse):
    n = a.shape[0]
    s = 1
    while s < n:
        a_sh = _shift_rows(a, s, 1.0, reverse)
        b_sh = _shift_rows(bv, s, 0.0, reverse)
        bv = a * b_sh + bv
        a = a * a_sh
        s *= 2
    return a, bv


def _shift_rows(x, s, fill, reverse):
    n = x.shape[0]
    if s % V7X_SUBLANES == 0:
        pad = jnp.full((s,) + x.shape[1:], fill, x.dtype)
        return jnp.concatenate([x[s:], pad], axis=0) if reverse else jnp.concatenate([pad, x[:n - s]], axis=0)
    row = lax.broadcasted_iota(jnp.int32, x.shape, 0)
    if reverse:
        return jnp.where(row < (n - s), pltpu.roll(x, n - s, 0), fill)
    return jnp.where(row >= s, pltpu.roll(x, s, 0), fill)


def _cumsum_rows(x, reverse):
    n = x.shape[0]
    s = 1
    while s < n:
        x = x + _shift_rows(x, s, 0.0, reverse)
        s *= 2
    return x


def _gelu_tanh(x):
    return 0.5 * x * (1.0 + jnp.tanh(math.sqrt(2.0 / math.pi) * (x + 0.044715 * (x * x * x))))


def _lru_body(xl_ref, xc_ref, gl_ref, gc_ref, cw_ref, cb_ref, wa_ref, ba_ref, wi_ref, bi_ref, lam_ref,
              yl_ref, yc_ref, cv_l, cv_c, hf_l, hf_c, *, seq, ctx, chunk, conv_chunk):
    pad = V7X_SUBLANES
    zeros8 = jnp.zeros((pad, V7X_LANES), F32)

    def conv_into(src_ref, dst, n, step):
        for c0 in range(0, n, step):
            lo = zeros8 if c0 == 0 else src_ref[c0 - pad:c0, :].astype(F32)
            hi = zeros8 if c0 + step >= n else src_ref[c0 + step:c0 + step + pad, :].astype(F32)
            ext = jnp.concatenate([lo, src_ref[c0:c0 + step, :].astype(F32), hi], axis=0)
            out = cb_ref[...]
            for kk in range(CONV_W):
                off = pad + kk - CONV_PAD_L
                out = out + ext[off:off + step, :] * cw_ref[kk:kk + 1, :]
            dst[c0:c0 + step, :] = out

    conv_into(xl_ref, cv_l, seq, conv_chunk)
    conv_into(xc_ref, cv_c, ctx, min(ctx, conv_chunk))

    for d, reverse in enumerate((False, True)):
        lam = lam_ref[d:d + 1, :]
        neg_sp = -LRU_C * (jnp.maximum(-lam, 0.0) + jnp.log(1.0 + jnp.exp(-jnp.abs(lam))))
        wa = wa_ref[d, 0]
        wi = wi_ref[d, 0]
        ba = ba_ref[d:d + 1, :]
        bi = bi_ref[d:d + 1, :]

        def block(x, carry):
            xg = x.astype(BF16)
            r = jax.nn.sigmoid(jnp.dot(xg, wa, preferred_element_type=F32) + ba)
            i = jax.nn.sigmoid(jnp.dot(xg, wi, preferred_element_type=F32) + bi)
            a = jnp.exp(neg_sp * r)
            bv = jnp.sqrt(1.0 - a * a) * (i * x)
            a_cum, h = _scan_rows(a, bv, reverse)
            h = a_cum * carry + h
            n = x.shape[0]
            new_carry = h[0:1, :] if reverse else h[n - 1:n, :]
            return h, new_carry

        def emit(dst_ref, fwd_scr, g_ref, start, n, h):
            if not reverse:
                fwd_scr[pl.ds(start, n), :] = h
            else:
                tot = fwd_scr[pl.ds(start, n), :] + h
                g = g_ref[pl.ds(start, n), :].astype(F32)
                dst_ref[pl.ds(start, n), :] = (tot * _gelu_tanh(g)).astype(dst_ref.dtype)

        carry = jnp.zeros((1, V7X_LANES), F32)
        cchunk = min(ctx, chunk)
        n_cc = ctx // cchunk
        order = range(n_cc - 1, -1, -1) if reverse else range(n_cc)
        lat_init = None
        for c in order:
            h, carry = block(cv_c[c * cchunk:(c + 1) * cchunk, :], carry)
            if lat_init is None:
                lat_init = h[cchunk - 1:cchunk, :] if reverse else h[0:1, :]
            emit(yc_ref, hf_c, gc_ref, c * cchunk, cchunk, h)
        carry = lat_init

        n_lc = seq // chunk

        def body(it, carry):
            c = (n_lc - 1 - it) if reverse else it
            start = pl.multiple_of(c * chunk, chunk)
            h, carry = block(cv_l[pl.ds(start, chunk), :], carry)
            emit(yl_ref, hf_l, gl_ref, start, chunk, h)
            return carry

        lax.fori_loop(0, n_lc, body, carry)


def _rglru(p, xb_col, gb_col, width, conv_w, conv_b, w_a, b_a, w_i, b_i, lam, rows):
    b, t, tc = rows.B, rows.T, rows.TC
    ln = V7X_LANES
    groups = width // ln
    ctx_blk0 = rows.n_lat // tc
    chunk = _pick(t, 256, V7X_SUBLANES)
    conv_chunk = _pick(t, 1024, V7X_SUBLANES)
    vec = lambda rws: pl.BlockSpec((rws, ln), lambda bb, g: (0, g))
    est = 2 * (t + tc) * ln * (2 + 2 + 2) + 2 * (t + tc) * ln * 4 + 64 * chunk * ln * 4
    return pl.pallas_call(
        functools.partial(_lru_body, seq=t, ctx=tc, chunk=chunk, conv_chunk=conv_chunk),
        grid=(b, groups),
        in_specs=[
            pl.BlockSpec((t, ln), lambda bb, g: (bb, xb_col + g)),
            pl.BlockSpec((tc, ln), lambda bb, g: (ctx_blk0 + bb, xb_col + g)),
            pl.BlockSpec((t, ln), lambda bb, g: (bb, gb_col + g)),
            pl.BlockSpec((tc, ln), lambda bb, g: (ctx_blk0 + bb, gb_col + g)),
            vec(CONV_W), vec(1),
            pl.BlockSpec((2, 1, ln, ln), lambda bb, g: (0, g, 0, 0)), vec(2),
            pl.BlockSpec((2, 1, ln, ln), lambda bb, g: (0, g, 0, 0)), vec(2),
            vec(2),
        ],
        out_specs=[
            pl.BlockSpec((t, ln), lambda bb, g: (bb, g)),
            pl.BlockSpec((tc, ln), lambda bb, g: (bb, g)),
        ],
        out_shape=[jax.ShapeDtypeStruct((b * t, width), BF16),
                   jax.ShapeDtypeStruct((b * tc, width), BF16)],
        scratch_shapes=[pltpu.VMEM((t, ln), F32), pltpu.VMEM((tc, ln), F32),
                        pltpu.VMEM((t, ln), F32), pltpu.VMEM((tc, ln), F32)],
        compiler_params=_cparams(("arbitrary", "arbitrary"), est),
    )(p, p, p, p, conv_w, conv_b.reshape(1, width), w_a, b_a, w_i, b_i, lam)


def _log_sigmoid(x):
    return jnp.minimum(x, 0.0) - jnp.log(1.0 + jnp.exp(-jnp.abs(x)))


def _linattn_body(*refs, kind, reverse, tb, dk, dv):
    it = iter(refs)
    q_ref, k_ref, v_ref = next(it), next(it), next(it)
    if kind == "ret":
        cos_ref, sin_ref, dec_ref = next(it), next(it), next(it)
    else:
        ga_ref, wg_ref, bg_ref = next(it), next(it), next(it)
    if reverse:
        of_ref, g_ref, gain_ref = next(it), next(it), next(it)
    o_ref, st = next(it), next(it)
    hp = st.shape[0]

    @pl.when(pl.program_id(2) == 0)
    def _():
        st[...] = jnp.zeros(st.shape, F32)

    n_ch = tb // CHUNK
    ri = lax.broadcasted_iota(jnp.int32, (CHUNK, CHUNK), 0)
    ci = lax.broadcasted_iota(jnp.int32, (CHUNK, CHUNK), 1)
    mask = (ci >= ri) if reverse else (ci <= ri)
    nt = (((1,), (1,)), ((), ()))
    tn = (((0,), (0,)), ((), ()))
    pos = lax.broadcasted_iota(jnp.int32, (tb, 1), 0) % CHUNK
    n_terms = ((CHUNK - pos) if reverse else (pos + 1)).astype(F32)
    for hh in range(hp):
        q = q_ref[:, hh * dk:(hh + 1) * dk].astype(F32) * (1.0 if kind == "ret" else dk ** -0.5)
        k = k_ref[:, hh * dk:(hh + 1) * dk].astype(F32) * (dk ** -0.5 if kind == "ret" else 1.0)
        if kind == "ret":
            c, s = cos_ref[...], sin_ref[...]
            hd = dk // 2
            q = jnp.concatenate([q[:, :hd] * c - q[:, hd:] * s, q[:, :hd] * s + q[:, hd:] * c], axis=1)
            k = jnp.concatenate([k[:, :hd] * c - k[:, hd:] * s, k[:, :hd] * s + k[:, hd:] * c], axis=1)
            bcum_all = n_terms * dec_ref[hh]
        else:
            z = jnp.dot(ga_ref[...], wg_ref[0, :, hh * dk:(hh + 1) * dk], preferred_element_type=F32)
            la = _log_sigmoid(z + bg_ref[0, :, hh * dk:(hh + 1) * dk]) * (1.0 / GLA_TAU)
        outs = [None] * n_ch
        for c in (range(n_ch - 1, -1, -1) if reverse else range(n_ch)):
            sl = slice(c * CHUNK, (c + 1) * CHUNK)
            bcum = bcum_all[sl] if kind == "ret" else _cumsum_rows(la[sl], reverse)
            bend = bcum[0:1, :] if reverse else bcum[CHUNK - 1:CHUNK, :]
            qe = (q[sl] * jnp.exp(bcum)).astype(BF16)
            ke = (k[sl] * jnp.exp(-bcum)).astype(BF16)
            ks = (k[sl] * jnp.exp(bend - bcum)).astype(BF16)
            vc = v_ref[sl, hh * dv:(hh + 1) * dv]
            att = lax.dot_general(qe, ke, nt, preferred_element_type=F32)
            att = jnp.where(mask, att, 0.0).astype(BF16)
            o = jnp.dot(att, vc, preferred_element_type=F32)
            o = o + lax.dot_general(qe, st[hh].astype(BF16), nt, preferred_element_type=F32)
            st[hh] = st[hh] * jnp.exp(bend) + lax.dot_general(vc, ks, tn, preferred_element_type=F32)
            outs[c] = o
        o = jnp.concatenate(outs, axis=0)
        cs = slice(hh * dv, (hh + 1) * dv)
        if not reverse:
            o_ref[:, cs] = o
        else:
            y = of_ref[:, cs] + o
            ms = jnp.mean(y * y, axis=-1, keepdims=True)
            y = (y * lax.rsqrt(ms + EPS)) * gain_ref[:, cs]
            g = g_ref[:, cs].astype(F32)
            o_ref[:, cs] = (y * (g * jax.nn.sigmoid(g))).astype(o_ref.dtype)


def _linattn(p, cols, heads, kind, reverse, rows, extra, o_fwd=None, gain=None):
    b, t, tc = rows.B, rows.T, rows.TC
    dk, dv = RET_DK, RET_DV
    tb = tc
    assert t % tb == 0 and tb % CHUNK == 0
    nt = 1 + t // tb
    ctx_blk0 = rows.n_lat // tb
    per_seq = t // tb

    def row_blk(bb, tt):
        lat = (per_seq - tt) if reverse else (tt - 1)
        return jnp.where(tt == 0, ctx_blk0 + bb, bb * per_seq + lat)

    qc, kc, vc, gc = cols
    hp = math.gcd(heads, LINATTN_HEADS_PER_STEP)
    wk, wv = hp * dk, hp * dv
    assert all(o % wk == 0 for o in (qc, kc)) and all(o % wv == 0 for o in (vc, gc))
    in_specs = [
        pl.BlockSpec((tb, wk), lambda bb, h, tt: (row_blk(bb, tt), qc // wk + h)),
        pl.BlockSpec((tb, wk), lambda bb, h, tt: (row_blk(bb, tt), kc // wk + h)),
        pl.BlockSpec((tb, wv), lambda bb, h, tt: (row_blk(bb, tt), vc // wv + h)),
    ]
    args = [p, p, p]
    if kind == "ret":
        cos, sin, dec = extra
        tab = pl.BlockSpec((tb, dk // 2), lambda bb, h, tt: (row_blk(bb, tt), 0))
        in_specs += [tab, tab, pl.BlockSpec((hp, 1, dk), lambda bb, h, tt: (h, 0, 0))]
        args += [cos, sin, dec]
    else:
        ga_rows, wg, bg = extra
        in_specs += [
            pl.BlockSpec((tb, V7X_LANES), lambda bb, h, tt: (row_blk(bb, tt), 0)),
            pl.BlockSpec((1, V7X_LANES, wk), lambda bb, h, tt: (0, 0, h)),
            pl.BlockSpec((1, 1, wk), lambda bb, h, tt: (0, 0, h)),
        ]
        args += [ga_rows, wg, bg]
    if reverse:
        in_specs += [
            pl.BlockSpec((tb, wv), lambda bb, h, tt: (row_blk(bb, tt), h)),
            pl.BlockSpec((tb, wv), lambda bb, h, tt: (row_blk(bb, tt), gc // wv + h)),
            pl.BlockSpec((1, wv), lambda bb, h, tt: (0, h)),
        ]
        args += [o_fwd, p, gain]
    est = hp * (2 * tb * (2 * dk + dv) * 2 + 6 * tb * dv * 4 + 3 * dv * dk * 4 + 24 * tb * dk * 4 + 4 * tb * dv * 4)
    return pl.pallas_call(
        functools.partial(_linattn_body, kind=kind, reverse=reverse, tb=tb, dk=dk, dv=dv),
        grid=(b, heads // hp, nt),
        in_specs=in_specs,
        out_specs=pl.BlockSpec((tb, wv), lambda bb, h, tt: (row_blk(bb, tt), h)),
        out_shape=jax.ShapeDtypeStruct((rows.R, heads * dv), BF16 if reverse else F32),
        scratch_shapes=[pltpu.VMEM((hp, dv, dk), F32)],
        compiler_params=_cparams(("arbitrary", "arbitrary", "arbitrary"), est),
    )(*args)


def _pad_cols(w, n):
    return jnp.pad(w, ((0, 0), (0, n - w.shape[1])))


def _axial_angles(t, rot_dim):
    n_rows = t // GRID_W
    row = jnp.repeat(jnp.arange(n_rows, dtype=F32), GRID_W)
    col = jnp.tile(jnp.arange(GRID_W, dtype=F32), n_rows)
    n_freq = rot_dim // 4
    inv = ROPE_BASE ** (-jnp.arange(n_freq, dtype=F32) / n_freq)
    return jnp.concatenate([row[:, None] * inv, col[:, None] * inv], axis=-1)


def _row_tables(rows, rot_dim):
    ang = _axial_angles(rows.T, rot_dim)
    half = rot_dim // 2
    cos = jnp.concatenate([jnp.tile(jnp.cos(ang), (rows.B, 1)), jnp.ones((rows.n_ctx, half), F32)], axis=0)
    sin = jnp.concatenate([jnp.tile(jnp.sin(ang), (rows.B, 1)), jnp.zeros((rows.n_ctx, half), F32)], axis=0)
    return cos, sin


def _mla_tables(rows):
    cos, sin = _row_tables(rows, MLA_ROPE)
    half = MLA_ROPE // 2
    z = jnp.zeros((rows.R, V7X_LANES - MLA_ROPE), F32)
    zh = jnp.zeros((rows.R, half), F32)
    c = jnp.concatenate([cos, cos, z], axis=1)
    s1 = jnp.concatenate([-sin, zh, z], axis=1)
    s2 = jnp.concatenate([zh, sin, z], axis=1)
    return c, s1, s2


def _even_mixer(h, gain, mods, rows, n_out, w_in, q_norm, w_uq, kv_norm, w_ukv, conv_w, conv_b,
                w_a, b_a, w_i, b_i, lam, w_out, tables):
    q_rank, kv_rank, lru_w = q_norm.shape[0], kv_norm.shape[0], conv_w.shape[1]
    ln = V7X_LANES
    o_cq, o_ckv, o_kr, o_xb, o_gb = np.cumsum([0, q_rank, kv_rank, MLA_ROPE, lru_w]).tolist()
    w_main = jnp.concatenate([w_in[:, :o_kr], w_in[:, o_xb:]], axis=1).astype(BF16)
    w_kr = _pad_cols(w_in[:, o_kr:o_xb], ln).astype(BF16)
    p, kr_raw = _norm_matmul(h, gain, mods, 1, w_main, w_main.shape[1], w_kr, rows, 4 * V7X_MXU_DIM)
    c_cq, c_ckv, c_xb, c_gb = 0, q_rank, q_rank + kv_rank, q_rank + kv_rank + lru_w

    scale = (MLA_NOPE + MLA_ROPE) ** -0.5 * math.log2(math.e)
    wq = (w_uq * scale).reshape(q_rank, MLA_HEADS, MLA_NOPE + MLA_ROPE)
    wq = jnp.pad(wq, ((0, 0), (0, 0), (0, 2 * ln - MLA_NOPE - MLA_ROPE))).reshape(q_rank, MLA_HEADS * 2 * ln)
    assert c_cq % q_rank == 0 and c_ckv % kv_rank == 0
    q = _lowrank_up(p, c_cq // q_rank, q_rank, q_norm, wq.astype(BF16), tables, rows, MLA_HEADS, True)
    kv = _lowrank_up(p, c_ckv // kv_rank, kv_rank, kv_norm, w_ukv.astype(BF16), tables, rows, MLA_HEADS, False)
    kr = _kr_rope(kr_raw, 0, tables, rows)

    att = _attention(q, kv, kr, rows, MLA_HEADS, latent=True)
    att_c = _attention(q, kv, kr, rows, MLA_HEADS, latent=False)

    y_l, y_c = _rglru(p, c_xb // ln, c_gb // ln, lru_w, conv_w, conv_b,
                      w_a.astype(BF16), b_a, w_i.astype(BF16), b_i, lam, rows)
    n_att = MLA_HEADS * MLA_V
    assert MLA_HEADS * MLA_V == lru_w
    return _res_matmul([(att, att_c), (y_l, y_c)], w_out.astype(BF16), (), h, mods, 1, 1.0, rows, n_out)


def _odd_mixer(h, gain, mods, rows, n_out, w_in, ret_log_decay, ret_norm, gla_w_gate2, gla_b_gate, gla_norm,
               w_out, tables):
    ln = V7X_LANES
    sizes = [RET_HEADS * RET_DK, RET_HEADS * RET_DK, RET_HEADS * RET_DV, RET_HEADS * RET_DV,
             GLA_HEADS * GLA_DK, GLA_HEADS * GLA_DK, GLA_HEADS * GLA_DV, GLA_HEADS * GLA_DV]
    offs = np.cumsum([0] + sizes).tolist()
    rq, rk, rv, rg, gq, gk, gv, gr, ga = offs
    w_main = w_in.astype(BF16)
    w_ga = _pad_cols(w_in[:, ga:], ln).astype(BF16)
    p, ga_rows = _norm_matmul(h, gain, mods, 1, w_main, ga, w_ga, rows, 4 * V7X_MXU_DIM)

    cos, sin = tables
    ys = []
    for kind, heads, cols, norm in (("ret", RET_HEADS, (rq, rk, rv, rg), ret_norm),
                                    ("gla", GLA_HEADS, (gq, gk, gv, gr), gla_norm)):
        o_f = None
        for d, reverse in enumerate((False, True)):
            if kind == "ret":
                dec = jnp.broadcast_to(ret_log_decay[d][:, None, None], (heads, 1, RET_DK)).astype(F32)
                extra = (cos, sin, dec)
            else:
                wg = jnp.zeros((1, ln, heads * GLA_DK), F32).at[0, d * GLA_RANK:(d + 1) * GLA_RANK].set(gla_w_gate2[d])
                extra = (ga_rows, wg.astype(BF16), gla_b_gate[d].reshape(1, 1, heads * GLA_DK))
            out = _linattn(p, cols, heads, kind, reverse, rows, extra, o_fwd=o_f,
                           gain=norm.reshape(1, heads * RET_DV))
            if reverse:
                ys.append(out)
            else:
                o_f = out
    n_ret = RET_HEADS * RET_DV
    assert RET_HEADS * RET_DV == GLA_HEADS * GLA_DV
    return _res_matmul(ys, w_out.astype(BF16), (), h, mods, 1, 1.0, rows, n_out)


def kernel(x, c, ctx, c_ctx, ada_w, ada_b, norm_w, ffn_w_gate, ffn_w_up, ffn_w_down, ev_w_in, mla_q_norm, mla_w_uq, mla_kv_norm, mla_w_ukv, lru_conv_w, lru_conv_b, lru_w_a, lru_b_a, lru_w_i, lru_b_i, lru_lambda, ev_w_out, od_w_in, ret_log_decay, ret_norm, gla_w_gate2, gla_b_gate, gla_norm, od_w_out, final_norm_w):
    batch, seq, d = x.shape
    tc = ctx.shape[1]
    depth = ada_w.shape[0]
    rows = _Rows(batch, seq, tc)
    h = (x.reshape(batch * seq, d), ctx.reshape(batch * tc, d))

    n_sets = 1 + batch
    c_rows = jnp.concatenate([c_ctx[None, :], c], axis=0)
    c_rows = jnp.pad(c_rows, ((0, -n_sets % V7X_SUBLANES), (0, 0)))
    mods_all = _ada(c_rows, ada_w, ada_b)[:, :n_sets].reshape(depth, n_sets, 9, d)

    mla_tabs = _mla_tables(rows)
    ret_tabs = _row_tables(rows, RET_DK)
    w_down_bf = ffn_w_down.astype(BF16)

    for l in range(depth):
        mods = mods_all[l]
        n_out = rows.n_lat if l == depth - 1 else rows.R

        def ffn(hh, k, idx, n_rows):
            u = _norm_mod(hh, norm_w[l, k], mods, k, rows, BF16, n_rows)
            a = _ffn_up(u, ffn_w_gate, ffn_w_up, (l, idx), rows)
            return _res_matmul([a], w_down_bf, (l, idx), hh, mods, k, FFN_RES, rows, n_rows)

        h = ffn(h, 0, 0, rows.R)
        if l % 2 == 0:
            e = l // 2
            h = _even_mixer(h, norm_w[l, 1], mods, rows, n_out, ev_w_in[e], mla_q_norm[e], mla_w_uq[e], mla_kv_norm[e],
                            mla_w_ukv[e], lru_conv_w[e], lru_conv_b[e], lru_w_a[e], lru_b_a[e],
                            lru_w_i[e], lru_b_i[e], lru_lambda[e], ev_w_out[e], mla_tabs)
        else:
            o = l // 2
            h = _odd_mixer(h, norm_w[l, 1], mods, rows, n_out, od_w_in[o], ret_log_decay[o], ret_norm[o], gla_w_gate2[o],
                           gla_b_gate[o], gla_norm[o], od_w_out[o], ret_tabs)
        h = ffn(h, 2, 1, n_out)

    out = _norm_mod(h, final_norm_w, mods_all[0], None, rows, F32, rows.n_lat)
    return out.reshape(batch, seq, d)
```
